```python
import jax, jax.numpy as jnp
from jax import lax
import numpy as np

D_MODEL = 1024
BATCH = 8
SEQ = 2048
DEPTH = 4

N_MEM = 256
N_MIXERS = 4
D_MIX = D_MODEL
GROUP = D_MIX // N_MIXERS
HEAD_DIM = 64
N_HEADS_G = GROUP // HEAD_DIM
SG_CHUNK = 128
RW_DECAY_LORA = 64
RW_AAA_LORA = 64
RW_GATE_LORA = 128
RW_LNX_EPS = 64e-5
GLA_DK = HEAD_DIM // 2
GLA_LORA = 16
GLA_TAU = 16.0
GLA_CHUNK = 64
GLA_NORM_EPS = 1e-5
FOX_BLOCK = 128
CA_HEADS = 4
CA_HEAD_DIM = D_MODEL // CA_HEADS
D_FF = 2816
CONV_W = 3
DN_ALPHA = (2.0 * DEPTH) ** 0.25
DN_BETA = (8.0 * DEPTH) ** -0.25
LN_EPS = 1e-5

SG_COLS = (GROUP, GROUP)
RW_COLS = (GROUP, GROUP, GROUP, RW_DECAY_LORA, RW_AAA_LORA, RW_GATE_LORA)
GLA_COLS = (N_HEADS_G * GLA_DK, N_HEADS_G * GLA_DK, GROUP, GROUP, GLA_LORA)
FOX_COLS = (GROUP, GROUP, GROUP, N_HEADS_G)
MIXER_COLS = (sum(SG_COLS), sum(RW_COLS), sum(GLA_COLS), sum(FOX_COLS))
P_IN = sum(MIXER_COLS)

kernel_name = "hybrid_sg_rwkv7_gla_fox_deepnorm_trunk"


def _split_cols(z, widths):
    outs, off = [], 0
    for w in widths:
        outs.append(z[..., off:off + w])
        off += w
    return outs


def _layernorm(x, g, b, eps=LN_EPS):
    xf = x.astype(jnp.float32)
    mu = jnp.mean(xf, axis=-1, keepdims=True)
    var = jnp.mean(jnp.square(xf - mu), axis=-1, keepdims=True)
    return (((xf - mu) * lax.rsqrt(var + eps)) * g + b).astype(x.dtype)


def _token_shift(z):
    return jnp.pad(z[:, :-1], ((0, 0), (1, 0), (0, 0)))


def _spatial_gating(u, v, ln_g, ln_b, w_s, b_s):
    B, S, _ = u.shape
    nc = S // SG_CHUNK
    v = v.reshape(B, nc, SG_CHUNK, N_HEADS_G, HEAD_DIM)
    v = _layernorm(v, ln_g.reshape(N_HEADS_G, HEAD_DIM), ln_b.reshape(N_HEADS_G, HEAD_DIM))
    causal = jnp.tril(jnp.ones((SG_CHUNK, SG_CHUNK), dtype=bool))
    w = jnp.where(causal, w_s, jnp.zeros((), w_s.dtype))
    s = jnp.einsum('hts,bnshd->bnthd', w, v) + b_s.T[None, None, :, :, None]
    return u * s.reshape(B, S, GROUP)


def _rwkv7_scan(r, decay, k, v, kk, a):
    B, S, H, N = r.shape
    xs = tuple(jnp.moveaxis(t, 1, 0) for t in (r, decay, k, v, kk, a))

    def step(state, inp):
        r_t, w_t, k_t, v_t, kk_t, a_t = inp
        sa = jnp.einsum('bhvk,bhk->bhv', state, -kk_t)
        state = (state * w_t[:, :, None, :]
                 + sa[..., None] * (kk_t * a_t)[:, :, None, :]
                 + v_t[..., None] * k_t[:, :, None, :])
        y = jnp.einsum('bhvk,bhk->bhv', state, r_t)
        return state, y

    s0 = jnp.zeros((B, H, N, N), jnp.float32)
    _, ys = lax.scan(step, s0, xs)
    return jnp.moveaxis(ys, 0, 1)


def _rwkv7(r, k, v, wd, ad, gd, w0, w2, a0, a2, g2, k_k, k_a, r_k, lnx_g, lnx_b):
    B, S, _ = r.shape
    w = -jax.nn.softplus(-(w0 + jnp.tanh(wd) @ w2)) - 0.5
    decay = jnp.exp(-jnp.exp(w.astype(jnp.float32)))
    a = jax.nn.sigmoid(a0 + ad @ a2)
    g = jax.nn.sigmoid(gd) @ g2
    heads = lambda t: t.reshape(B, S, N_HEADS_G, HEAD_DIM).astype(jnp.float32)
    kk = heads(k * k_k)
    kk = kk / jnp.maximum(jnp.sqrt(jnp.sum(kk * kk, axis=-1, keepdims=True)), 1e-12)
    k = k * (1.0 + (a - 1.0) * k_a)
    rh, kh, vh, ah, dh = heads(r), heads(k), heads(v), heads(a), heads(decay)
    y = _rwkv7_scan(rh, dh, kh, vh, kk, ah)
    mu = jnp.mean(y, axis=-1, keepdims=True)
    var = jnp.mean(jnp.square(y - mu), axis=-1, keepdims=True)
    y = (y - mu) * lax.rsqrt(var + RW_LNX_EPS)
    y = y * lnx_g.reshape(N_HEADS_G, HEAD_DIM) + lnx_b.reshape(N_HEADS_G, HEAD_DIM)
    y = y + jnp.sum(rh * kh * r_k, axis=-1, keepdims=True) * vh
    return (y.reshape(B, S, GROUP) * g).astype(r.dtype)


def _gla(q, k, v, g, ad, a_up, a_b, norm_g):
    B, S, _ = q.shape
    C = GLA_CHUNK
    nc = S // C
    H = N_HEADS_G
    shape_k = (B, nc, C, H, GLA_DK)
    lg = jax.nn.log_sigmoid((ad @ a_up + a_b).astype(jnp.float32)) / GLA_TAU
    qf = q.astype(jnp.float32).reshape(shape_k) * GLA_DK ** -0.5
    kf = k.astype(jnp.float32).reshape(shape_k)
    vf = v.astype(jnp.float32).reshape(B, nc, C, H, HEAD_DIM)
    b = jnp.cumsum(lg.reshape(shape_k), axis=2)
    b_ref = b[:, :, C // 2][:, :, None]
    b_last = b[:, :, -1]
    causal = jnp.tril(jnp.ones((C, C), dtype=bool))
    att = jnp.einsum('bnthd,bnshd->bnhts', qf * jnp.exp(b - b_ref), kf * jnp.exp(b_ref - b))
    att = jnp.where(causal, att, 0.0)
    o_intra = jnp.einsum('bnhts,bnshv->bnthv', att, vf)
    u = jnp.einsum('bnshd,bnshv->bnhdv', kf * jnp.exp(b_last[:, :, None] - b), vf)
    dec = jnp.exp(b_last)

    def step(state, inp):
        dec_n, u_n = inp
        return dec_n[..., None] * state + u_n, state

    s0 = jnp.zeros((B, H, GLA_DK, HEAD_DIM), jnp.float32)
    _, s_prev = lax.scan(step, s0, (jnp.moveaxis(dec, 1, 0), jnp.moveaxis(u, 1, 0)))
    s_prev = jnp.moveaxis(s_prev, 0, 1)
    o_inter = jnp.einsum('bnthd,bnhdv->bnthv', qf * jnp.exp(b), s_prev)
    o = (o_intra + o_inter).reshape(B, S, H, HEAD_DIM)
    o = o * lax.rsqrt(jnp.mean(o * o, axis=-1, keepdims=True) + GLA_NORM_EPS)
    o = o.reshape(B, S, GROUP) * norm_g * jax.nn.silu(g.astype(jnp.float32))
    return o.astype(q.dtype)


def _forgetting_attention(q, k, v, f_logit, f_bias):
    B, S, _ = q.shape
    H = N_HEADS_G
    heads = lambda t: jnp.moveaxis(t.reshape(B, S, H, HEAD_DIM), 2, 1).astype(jnp.float32)
    qh, kh, vh = heads(q), heads(k), heads(v)
    log_f = jax.nn.log_sigmoid((f_logit + f_bias).astype(jnp.float32))
    c = jnp.transpose(jnp.cumsum(log_f, axis=1), (0, 2, 1))
    scale = HEAD_DIM ** -0.5
    outs = []
    for i in range(S // FOX_BLOCK):
        q0 = i * FOX_BLOCK
        end = q0 + FOX_BLOCK
        logits = jnp.einsum('bhtd,bhsd->bhts', qh[:, :, q0:end], kh[:, :, :end]) * scale
        logits = logits + c[:, :, q0:end, None] - c[:, :, None, :end]
        causal = jnp.arange(end)[None, :] <= (q0 + jnp.arange(FOX_BLOCK))[:, None]
        p = jax.nn.softmax(jnp.where(causal, logits, -jnp.inf), axis=-1)
        outs.append(jnp.einsum('bhts,bhsd->bhtd', p, vh[:, :, :end]))
    o = jnp.concatenate(outs, axis=2)
    return jnp.moveaxis(o, 1, 2).reshape(B, S, GROUP).astype(q.dtype)


def _memory_attention(x, memn, wq, wk, wv, wo):
    B, S, _ = x.shape
    M = memn.shape[1]
    q = (x @ wq).reshape(B, S, CA_HEADS, CA_HEAD_DIM)
    k = (memn @ wk).reshape(B, M, CA_HEADS, CA_HEAD_DIM)
    v = (memn @ wv).reshape(B, M, CA_HEADS, CA_HEAD_DIM)
    logits = jnp.einsum('bthd,bmhd->bhtm', q, k).astype(jnp.float32) * CA_HEAD_DIM ** -0.5
    p = jax.nn.softmax(logits, axis=-1).astype(x.dtype)
    o = jnp.einsum('bhtm,bmhd->bthd', p, v).reshape(B, S, D_MODEL)
    return o @ wo


def _conv_ffn(x, w_up, b_up, w_conv, b_conv, w_down):
    h = x @ w_up + b_up
    h = lax.conv_general_dilated(h, w_conv[:, None, :], window_strides=(1,),
                                 padding=[(CONV_W - 1, 0)],
                                 dimension_numbers=('NWC', 'WIO', 'NWC'),
                                 feature_group_count=h.shape[-1]) + b_conv
    gate, val = jnp.split(h, 2, axis=-1)
    return (jax.nn.gelu(gate, approximate=False) * val) @ w_down


def setup_inputs(seed: int = 0) -> dict:
    key = jax.random.key(seed)
    ks = iter(jax.random.split(key, 48))
    L, D, H = DEPTH, D_MODEL, N_HEADS_G
    nrm = lambda shape, s: s * jax.random.normal(next(ks), shape, jnp.float32)
    uni = lambda shape, lo, hi: jax.random.uniform(next(ks), shape, jnp.float32, lo, hi)
    gain = lambda shape: 1.0 + nrm(shape, 0.02)
    return {
        "x": nrm((BATCH, SEQ, D), 1.0),
        "mem": nrm((BATCH, N_MEM, D), 1.0),
        "mem_ln_g": gain((D,)),
        "mem_ln_b": nrm((D,), 0.02),
        "w_in": nrm((L, D, P_IN), D ** -0.5),
        "w_out": nrm((L, D_MIX, D), D_MIX ** -0.5 * DN_BETA),
        "sg_ln_g": gain((L, GROUP)),
        "sg_ln_b": nrm((L, GROUP), 0.02),
        "sg_w": nrm((L, H, SG_CHUNK, SG_CHUNK), 0.5 * SG_CHUNK ** -0.5),
        "sg_b": 1.0 + nrm((L, H, SG_CHUNK), 0.02),
        "rw_mu": uni((L, sum(RW_COLS)), 0.0, 1.0),
        "rw_w0": uni((L, GROUP), -6.0, 1.0),
        "rw_w2": nrm((L, RW_DECAY_LORA, GROUP), 0.1 * RW_DECAY_LORA ** -0.5),
        "rw_a0": nrm((L, GROUP), 0.1),
        "rw_a2": nrm((L, RW_AAA_LORA, GROUP), 0.1 * RW_AAA_LORA ** -0.5),
        "rw_g2": nrm((L, RW_GATE_LORA, GROUP), RW_GATE_LORA ** -0.5),
        "rw_kk": 0.85 + nrm((L, GROUP), 0.02),
        "rw_ka": gain((L, GROUP)),
        "rw_rk": nrm((L, H, HEAD_DIM), 0.1),
        "rw_lnx_g": gain((L, GROUP)),
        "rw_lnx_b": nrm((L, GROUP), 0.02),
        "gla_a_up": nrm((L, GLA_LORA, H * GLA_DK), GLA_LORA ** -0.5),
        "gla_a_b": nrm((L, H * GLA_DK), 0.1),
        "gla_norm_g": gain((L, GROUP)),
        "fox_fb": uni((L, H), 1.0, 4.0),
        "ln1_g": gain((L, D)),
        "ln1_b": nrm((L, D), 0.02),
        "ca_wq": nrm((L, D, D), D ** -0.5),
        "ca_wk": nrm((L, D, D), D ** -0.5),
        "ca_wv": nrm((L, D, D), D ** -0.5),
        "ca_wo": nrm((L, D, D), D ** -0.5 * DN_BETA),
        "ln2_g": gain((L, D)),
        "ln2_b": nrm((L, D), 0.02),
        "ffn_up": nrm((L, D, 2 * D_FF), D ** -0.5),
        "ffn_up_b": nrm((L, 2 * D_FF), 0.02),
        "ffn_conv": nrm((L, CONV_W, 2 * D_FF), CONV_W ** -0.5),
        "ffn_conv_b": nrm((L, 2 * D_FF), 0.02),
        "ffn_down": nrm((L, D_FF, D), D_FF ** -0.5 * DN_BETA),
        "ln3_g": gain((L, D)),
        "ln3_b": nrm((L, D), 0.02),
    }


def reference(x, mem, mem_ln_g, mem_ln_b, w_in, w_out, sg_ln_g, sg_ln_b, sg_w, sg_b,
              rw_mu, rw_w0, rw_w2, rw_a0, rw_a2, rw_g2, rw_kk, rw_ka, rw_rk, rw_lnx_g, rw_lnx_b,
              gla_a_up, gla_a_b, gla_norm_g, fox_fb, ln1_g, ln1_b,
              ca_wq, ca_wk, ca_wv, ca_wo, ln2_g, ln2_b,
              ffn_up, ffn_up_b, ffn_conv, ffn_conv_b, ffn_down, ln3_g, ln3_b):
    memn = _layernorm(mem, mem_ln_g, mem_ln_b)
    for l in range(DEPTH):
        z = x @ w_in[l]
        z_sg, z_rw, z_gla, z_fox = _split_cols(z, MIXER_COLS)
        sg_u, sg_v = _split_cols(z_sg, SG_COLS)
        y_a = _spatial_gating(sg_u, sg_v, sg_ln_g[l], sg_ln_b[l], sg_w[l], sg_b[l])
        z_rw = z_rw + rw_mu[l] * (_token_shift(z_rw) - z_rw)
        rw_r, rw_k, rw_v, rw_wd, rw_ad, rw_gd = _split_cols(z_rw, RW_COLS)
        y_b = _rwkv7(rw_r, rw_k, rw_v, rw_wd, rw_ad, rw_gd, rw_w0[l], rw_w2[l], rw_a0[l], rw_a2[l],
                     rw_g2[l], rw_kk[l], rw_ka[l], rw_rk[l], rw_lnx_g[l], rw_lnx_b[l])
        gl_q, gl_k, gl_v, gl_g, gl_ad = _split_cols(z_gla, GLA_COLS)
        y_c = _gla(gl_q, gl_k, gl_v, gl_g, gl_ad, gla_a_up[l], gla_a_b[l], gla_norm_g[l])
        fx_q, fx_k, fx_v, fx_f = _split_cols(z_fox, FOX_COLS)
        y_d = _forgetting_attention(fx_q, fx_k, fx_v, fx_f, fox_fb[l])
        mix = jnp.concatenate([y_a, y_b, y_c, y_d], axis=-1) @ w_out[l]
        x = _layernorm(DN_ALPHA * x + mix, ln1_g[l], ln1_b[l])
        ca = _memory_attention(x, memn, ca_wq[l], ca_wk[l], ca_wv[l], ca_wo[l])
        x = _layernorm(DN_ALPHA * x + ca, ln2_g[l], ln2_b[l])
        ff = _conv_ffn(x, ffn_up[l], ffn_up_b[l], ffn_conv[l], ffn_conv_b[l], ffn_down[l])
        x = _layernorm(DN_ALPHA * x + ff, ln3_g[l], ln3_b[l])
    return x
```

```python
import functools

import jax
import jax.numpy as jnp
from jax import lax
from jax.experimental import pallas as pl
from jax.experimental.pallas import tpu as pltpu

F32 = jnp.float32
BF16 = jnp.bfloat16

D_MODEL = 1024
DEPTH = 4
N_MEM = 256
HEAD_DIM = 64
N_HEADS = 4
GROUP = 256
SG_CHUNK = 128
RW_LNX_EPS = 64e-5
RW_CHUNK = 64
GLA_DK = 32
GLA_TAU = 16.0
GLA_CHUNK = 64
GLA_NORM_EPS = 1e-5
CA_HEADS = 4
CA_HEAD_DIM = 256
D_FF = 2816
DN_ALPHA = (2.0 * DEPTH) ** 0.25
LN_EPS = 1e-5

P_PACK = 12 * 256 + 128
MISC_BLOCK = 24
MISC_FOX_LANE = 16

ROW_TILE = 512
FFN_COL_TILE = 256
VMEM_LIMIT = 48 * 1024 * 1024


def _cparams(*sem):
    return pltpu.CompilerParams(dimension_semantics=sem, vmem_limit_bytes=VMEM_LIMIT)


def _dot(a, b):
    return jnp.dot(a.astype(BF16), b.astype(BF16), preferred_element_type=F32)


def _dot_nt(a, b):
    return lax.dot_general(a.astype(BF16), b.astype(BF16), (((1,), (1,)), ((), ())),
                           preferred_element_type=F32)


def _dot_tn(a, b):
    return lax.dot_general(a.astype(BF16), b.astype(BF16), (((0,), (0,)), ((), ())),
                           preferred_element_type=F32)


def _split2(a):
    hi = a.astype(BF16)
    lo = (a - hi.astype(F32)).astype(BF16)
    return hi, lo


def _dot3(a, b):
    ah, al = _split2(a)
    bh, bl = _split2(b)
    return (jnp.dot(ah, bh, preferred_element_type=F32)
            + (jnp.dot(ah, bl, preferred_element_type=F32) + jnp.dot(al, bh, preferred_element_type=F32)))


def _group_ones(n, shift):
    r = lax.broadcasted_iota(jnp.int32, (n, n), 0) >> shift
    c = lax.broadcasted_iota(jnp.int32, (n, n), 1) >> shift
    return jnp.where(r == c, 1.0, 0.0).astype(BF16)


def _gsum(x, ones):
    hi, lo = _split2(x)
    return jnp.dot(hi, ones, preferred_element_type=F32) + jnp.dot(lo, ones, preferred_element_type=F32)


def _lane_head(width, shift):
    return lax.broadcasted_iota(jnp.int32, (1, width), 1) >> shift


def _softplus(x):
    return jnp.maximum(x, 0.0) + jnp.log1p(jnp.exp(-jnp.abs(x)))


def _log_sigmoid(x):
    return jnp.minimum(x, 0.0) - jnp.log1p(jnp.exp(-jnp.abs(x)))


def _sigmoid(x):
    return 1.0 / (1.0 + jnp.exp(-x))


def _layernorm(x, g, b):
    mu = jnp.mean(x, axis=-1, keepdims=True)
    d = x - mu
    var = jnp.mean(d * d, axis=-1, keepdims=True)
    return d * lax.rsqrt(var + LN_EPS) * g + b


def _seg_cumsum(x, seg):
    pos = lax.broadcasted_iota(jnp.int32, x.shape, 0) & (seg - 1)
    k = 1
    while k < seg:
        x = x + jnp.where(pos >= k, pltpu.roll(x, k, 0), 0.0)
        k *= 2
    return x


def _stack_heads(x, lane_head):
    return jnp.concatenate([jnp.where(lane_head == h, x, 0.0) for h in range(N_HEADS)], axis=0)


def _unstack_heads(x, c):
    return (x[0:c] + x[c:2 * c]) + (x[2 * c:3 * c] + x[3 * c:4 * c])


def _inproj_kernel(x_ref, w_ref, o_ref):
    o_ref[...] = jnp.dot(x_ref[...].astype(BF16), w_ref[...], preferred_element_type=F32)


def _inproj(x, w_pack, layer):
    t = x.shape[0]
    return pl.pallas_call(
        _inproj_kernel,
        grid=(t // ROW_TILE,),
        in_specs=[pl.BlockSpec((ROW_TILE, D_MODEL), lambda i: (i, 0)),
                  pl.BlockSpec((None, D_MODEL, P_PACK), lambda i: (layer, 0, 0))],
        out_specs=pl.BlockSpec((ROW_TILE, P_PACK), lambda i: (i, 0)),
        out_shape=jax.ShapeDtypeStruct((t, P_PACK), F32),
        compiler_params=_cparams("parallel"),
        name="inproj",
    )(x, w_pack)


def _sg_kernel(u_ref, v_ref, g_ref, b_ref, w_ref, sb_ref, o_ref):
    ones = _group_ones(GROUP, 6)
    lane_head = _lane_head(GROUP, 6)
    v = v_ref[...]
    mean = _gsum(v, ones) * (1.0 / HEAD_DIM)
    d = v - mean
    var = _gsum(d * d, ones) * (1.0 / HEAD_DIM)
    vn = d * lax.rsqrt(var + LN_EPS) * g_ref[...] + b_ref[...]
    row = lax.broadcasted_iota(jnp.int32, (SG_CHUNK, SG_CHUNK), 0)
    col = lax.broadcasted_iota(jnp.int32, (SG_CHUNK, SG_CHUNK), 1)
    wm = [jnp.where(col <= row, w_ref[h], 0.0).astype(BF16) for h in range(N_HEADS)]
    for c in range(ROW_TILE // SG_CHUNK):
        sl = slice(c * SG_CHUNK, (c + 1) * SG_CHUNK)
        vc = vn[sl]
        s = sb_ref[...]
        for h in range(N_HEADS):
            s = s + jnp.dot(wm[h], jnp.where(lane_head == h, vc, 0.0).astype(BF16),
                            preferred_element_type=F32)
        o_ref[sl, :] = u_ref[sl, :] * s


def _spatial_gating(z, ln_g, ln_b, w_s, sbias, layer):
    t = z.shape[0]
    return pl.pallas_call(
        _sg_kernel,
        grid=(t // ROW_TILE,),
        in_specs=[pl.BlockSpec((ROW_TILE, GROUP), lambda i: (i, 0)),
                  pl.BlockSpec((ROW_TILE, GROUP), lambda i: (i, 1)),
                  pl.BlockSpec((None, 1, GROUP), lambda i: (layer, 0, 0)),
                  pl.BlockSpec((None, 1, GROUP), lambda i: (layer, 0, 0)),
                  pl.BlockSpec((None, N_HEADS, SG_CHUNK, SG_CHUNK), lambda i: (layer, 0, 0, 0)),
                  pl.BlockSpec((None, SG_CHUNK, GROUP), lambda i: (layer, 0, 0))],
        out_specs=pl.BlockSpec((ROW_TILE, GROUP), lambda i: (i, 0)),
        out_shape=jax.ShapeDtypeStruct((t, GROUP), F32),
        compiler_params=_cparams("parallel"),
        name="spatial_gating",
    )(z, z, ln_g, ln_b, w_s, sbias)


def _tri_inverse(a, same_head, rr, cc):
    n = a.shape[0]
    eye = jnp.where(lax.broadcasted_iota(jnp.int32, (n, n), 0)
                    == lax.broadcasted_iota(jnp.int32, (n, n), 1), 1.0, 0.0)
    b16 = (rr >> 4) == (cc >> 4)
    b32 = (rr >> 5) == (cc >> 5)
    d = jnp.where(same_head & b16, a, 0.0)
    t = eye + d
    x = d
    for _ in range(3):
        x = _dot3(x, x)
        t = t + _dot3(t, x)
    e = jnp.where(same_head & b32 & jnp.logical_not(b16), a, 0.0)
    t = t + _dot3(_dot3(t, e), t)
    f = jnp.where(same_head & jnp.logical_not(b32), a, 0.0)
    t = t + _dot3(_dot3(t, f), t)
    return t


def _rwkv_kernel(r_ref, k_ref, v_ref, lo_ref, mu_ref, w0_ref, a0_ref, w2_ref, a2_ref, g2_ref,
                 kk_ref, ka_ref, rk_ref, lng_ref, lnb_ref, o_ref,
                 prev_ref, st_ref, at_s, bt_s, kt_s, rt_s, v_s, wc_s, y_s):
    c_len = RW_CHUNK
    n_chunks = ROW_TILE // c_len

    @pl.when(pl.program_id(1) == 0)
    def _():
        prev_ref[...] = jnp.zeros_like(prev_ref)
        st_ref[...] = jnp.zeros_like(st_ref)

    ones = _group_ones(GROUP, 6)
    lane_head = _lane_head(GROUP, 6)
    row0 = lax.broadcasted_iota(jnp.int32, (ROW_TILE, GROUP), 0) == 0

    def shifted(ref, j):
        z = ref[...]
        prev = prev_ref[j]
        zs = jnp.where(row0, prev[7:8, :], pltpu.roll(z, 1, 0))
        prev_ref[j] = z[ROW_TILE - 8:ROW_TILE, :]
        return z + mu_ref[j] * (zs - z)

    r = shifted(r_ref, 0)
    k = shifted(k_ref, 1)
    v = shifted(v_ref, 2)
    lo = shifted(lo_ref, 3)

    wlog = -_softplus(-(w0_ref[...] + _dot(jnp.tanh(lo), w2_ref[...]))) - 0.5
    ld = -jnp.exp(wlog)
    a = _sigmoid(a0_ref[...] + _dot(lo, a2_ref[...]))
    gate = _dot(_sigmoid(lo), g2_ref[...])
    kk = k * kk_ref[...]
    kk = kk / jnp.maximum(jnp.sqrt(_gsum(kk * kk, ones)), 1e-12)
    k = k * (1.0 + (a - 1.0) * ka_ref[...])
    bonus = _gsum(r * k * rk_ref[...], ones) * v

    cl = _seg_cumsum(ld, c_len)
    e_pos = jnp.exp(cl)
    e_neg = jnp.exp(-cl)
    at_s[...] = -kk * jnp.exp(cl - ld)
    bt_s[...] = kk * a * e_neg
    kt_s[...] = k * e_neg
    rt_s[...] = r * e_pos
    v_s[...] = v
    wc_s[...] = e_pos

    n4 = N_HEADS * c_len
    ri = lax.broadcasted_iota(jnp.int32, (n4, n4), 0)
    ci = lax.broadcasted_iota(jnp.int32, (n4, n4), 1)
    same_head = (ri >> 6) == (ci >> 6)
    rr = ri & (c_len - 1)
    cc = ci & (c_len - 1)
    strict = same_head & (rr > cc)
    incl = same_head & (rr >= cc)

    def chunk(c, carry):
        sl = pl.ds(pl.multiple_of(c * c_len, c_len), c_len)
        at = at_s[sl, :]
        bt = bt_s[sl, :]
        kt = kt_s[sl, :]
        rt = rt_s[sl, :]
        vc = v_s[sl, :]
        st = st_ref[...]
        a_stack = _stack_heads(at, lane_head)
        r_stack = _stack_heads(rt, lane_head)
        v_stack = _stack_heads(vc, lane_head)
        b_tile = jnp.concatenate([bt] * N_HEADS, axis=0)
        k_tile = jnp.concatenate([kt] * N_HEADS, axis=0)
        a_ab = jnp.where(strict, _dot_nt(a_stack, b_tile), 0.0)
        a_ak = jnp.where(strict, _dot_nt(a_stack, k_tile), 0.0)
        l_rb = jnp.where(incl, _dot_nt(r_stack, b_tile), 0.0)
        l_rk = jnp.where(incl, _dot_nt(r_stack, k_tile), 0.0)
        t_inv = _tri_inverse(a_ab, same_head, rr, cc)
        ah = _dot_nt(at, st)
        rh = _dot_nt(rt, st)
        rhs = _stack_heads(ah, lane_head) + _dot(a_ak, v_stack)
        u_stack = _dot(t_inv, rhs)
        y_stack = _dot(l_rb, u_stack) + _dot(l_rk, v_stack)
        y_s[sl, :] = rh + _unstack_heads(y_stack, c_len)
        u = _unstack_heads(u_stack, c_len)
        upd = _dot_tn(jnp.concatenate([u, vc], axis=0), jnp.concatenate([bt, kt], axis=0))
        wc = wc_s[pl.ds(pl.multiple_of(c * c_len + (c_len - 8), 8), 8), :]
        st_ref[...] = (st + jnp.where(same_head, upd, 0.0)) * wc[7:8, :]
        return carry

    lax.fori_loop(0, n_chunks, chunk, 0)

    y = y_s[...]
    mean = _gsum(y, ones) * (1.0 / HEAD_DIM)
    d = y - mean
    var = _gsum(d * d, ones) * (1.0 / HEAD_DIM)
    y = d * lax.rsqrt(var + RW_LNX_EPS) * lng_ref[...] + lnb_ref[...]
    o_ref[...] = (y + bonus) * gate


def _rwkv(z, p, layer, batch, seq):
    t = z.shape[0]
    tiles = seq // ROW_TILE
    row = lambda c: pl.BlockSpec((ROW_TILE, GROUP), lambda b, i, c=c: (b * tiles + i, c))
    vec = pl.BlockSpec((None, 1, GROUP), lambda b, i: (layer, 0, 0))
    mat = pl.BlockSpec((None, GROUP, GROUP), lambda b, i: (layer, 0, 0))
    scr = pltpu.VMEM((ROW_TILE, GROUP), F32)
    return pl.pallas_call(
        _rwkv_kernel,
        grid=(batch, tiles),
        in_specs=[row(2), row(3), row(4), row(5),
                  pl.BlockSpec((None, 4, 1, GROUP), lambda b, i: (layer, 0, 0, 0)),
                  vec, vec, mat, mat, mat, vec, vec, vec, vec, vec],
        out_specs=pl.BlockSpec((ROW_TILE, GROUP), lambda b, i: (b * tiles + i, 0)),
        out_shape=jax.ShapeDtypeStruct((t, GROUP), F32),
        scratch_shapes=[pltpu.VMEM((4, 8, GROUP), F32), pltpu.VMEM((GROUP, GROUP), F32),
                        scr, scr, scr, scr, scr, scr, scr],
        compiler_params=_cparams("parallel", "arbitrary"),
        name="rwkv7",
    )(z, z, z, z, p["rw_mu"], p["rw_w0"], p["rw_a0"], p["rw_w2"], p["rw_a2"], p["rw_g2"],
      p["rw_kk"], p["rw_ka"], p["rw_rk"], p["rw_lnx_g"], p["rw_lnx_b"])


def _gla_kernel(qk_ref, v_ref, g_ref, m_ref, aup_ref, ab_ref, ng_ref, o_ref,
                st_ref, qe_s, ke_s, k2_s, qi_s, dec_s, o_s):
    c_len = GLA_CHUNK
    n_chunks = ROW_TILE // c_len
    kw = N_HEADS * GLA_DK

    @pl.when(pl.program_id(1) == 0)
    def _():
        st_ref[...] = jnp.zeros_like(st_ref)

    lg = _log_sigmoid(_dot(m_ref[...], aup_ref[...]) + ab_ref[...]) * (1.0 / GLA_TAU)
    b = _seg_cumsum(lg, c_len).reshape(n_chunks, c_len, kw)
    b_mid = b[:, c_len // 2:c_len // 2 + 1, :]
    b_last = b[:, c_len - 1:c_len, :]
    q = (qk_ref[:, 0:kw] * (GLA_DK ** -0.5)).reshape(n_chunks, c_len, kw)
    k = qk_ref[:, kw:2 * kw].reshape(n_chunks, c_len, kw)
    qe_s[...] = (q * jnp.exp(b - b_mid)).reshape(ROW_TILE, kw)
    ke_s[...] = (k * jnp.exp(b_mid - b)).reshape(ROW_TILE, kw)
    k2_s[...] = (k * jnp.exp(b_last - b)).reshape(ROW_TILE, kw)
    qi_s[...] = (q * jnp.exp(b)).reshape(ROW_TILE, kw)
    dec_s[...] = jnp.exp(b).reshape(ROW_TILE, kw)

    k_head = _lane_head(kw, 5)
    v_head = _lane_head(GROUP, 6)
    causal = ((lax.broadcasted_iota(jnp.int32, (c_len, N_HEADS * c_len), 1) & (c_len - 1))
              <= lax.broadcasted_iota(jnp.int32, (c_len, N_HEADS * c_len), 0))
    same_head = ((lax.broadcasted_iota(jnp.int32, (GROUP, kw), 0) >> 6)
                 == (lax.broadcasted_iota(jnp.int32, (GROUP, kw), 1) >> 5))

    def chunk(c, carry):
        base = pl.multiple_of(c * c_len, c_len)
        sl = pl.ds(base, c_len)
        vc = v_ref[sl, :]
        st = st_ref[...]
        att = jnp.where(causal, _dot_nt(qe_s[sl, :], _stack_heads(ke_s[sl, :], k_head)), 0.0)
        o_s[sl, :] = _dot(att, _stack_heads(vc, v_head)) + _dot_nt(qi_s[sl, :], st)
        dec = dec_s[pl.ds(pl.multiple_of(base + (c_len - 8), 8), 8), :]
        st_ref[...] = st * dec[7:8, :] + jnp.where(same_head, _dot_tn(vc, k2_s[sl, :]), 0.0)
        return carry

    lax.fori_loop(0, n_chunks, chunk, 0)

    o = o_s[...]
    ms = _gsum(o * o, _group_ones(GROUP, 6)) * (1.0 / HEAD_DIM)
    g = g_ref[...]
    o_ref[...] = o * lax.rsqrt(ms + GLA_NORM_EPS) * ng_ref[...] * (g * _sigmoid(g))


def _gla(z, p, layer, batch, seq):
    t = z.shape[0]
    tiles = seq // ROW_TILE
    kw = N_HEADS * GLA_DK
    row = lambda c: pl.BlockSpec((ROW_TILE, GROUP), lambda b, i, c=c: (b * tiles + i, c))
    kscr = pltpu.VMEM((ROW_TILE, kw), F32)
    return pl.pallas_call(
        _gla_kernel,
        grid=(batch, tiles),
        in_specs=[row(6), row(7), row(8),
                  pl.BlockSpec((ROW_TILE, 128), lambda b, i: (b * tiles + i, MISC_BLOCK)),
                  pl.BlockSpec((None, 128, kw), lambda b, i: (layer, 0, 0)),
                  pl.BlockSpec((None, 1, kw), lambda b, i: (layer, 0, 0)),
                  pl.BlockSpec((None, 1, GROUP), lambda b, i: (layer, 0, 0))],
        out_specs=pl.BlockSpec((ROW_TILE, GROUP), lambda b, i: (b * tiles + i, 0)),
        out_shape=jax.ShapeDtypeStruct((t, GROUP), F32),
        scratch_shapes=[pltpu.VMEM((GROUP, kw), F32), kscr, kscr, kscr, kscr, kscr,
                        pltpu.VMEM((ROW_TILE, GROUP), F32)],
        compiler_params=_cparams("parallel", "arbitrary"),
        name="gla",
    )(z, z, z, z, p["gla_a_up"], p["gla_a_b"], p["gla_norm_g"])


def _fox_gate_kernel(z_ref, fb_ref, c_ref):
    c_ref[...] = _seg_cumsum(_log_sigmoid(z_ref[...] + fb_ref[...]), z_ref.shape[0])


def _fox_gate(z, fb, layer, batch, seq):
    return pl.pallas_call(
        _fox_gate_kernel,
        grid=(batch,),
        in_specs=[pl.BlockSpec((seq, 128), lambda b: (b, MISC_BLOCK)),
                  pl.BlockSpec((None, 1, 128), lambda b: (layer, 0, 0))],
        out_specs=pl.BlockSpec((seq, 128), lambda b: (b, 0)),
        out_shape=jax.ShapeDtypeStruct((batch * seq, 128), F32),
        compiler_params=_cparams("parallel"),
        name="fox_gate",
    )(z, fb)


def _fox_kernel(q_ref, k_ref, v_ref, cc_ref, cr_ref, o_ref, m_ref, l_ref, acc_ref):
    tq = ROW_TILE
    i = pl.program_id(1)
    j = pl.program_id(2)

    @pl.when(j == 0)
    def _():
        m_ref[...] = jnp.full_like(m_ref, -1e30)
        l_ref[...] = jnp.zeros_like(l_ref)
        acc_ref[...] = jnp.zeros_like(acc_ref)

    lane_head = _lane_head(GROUP, 6)

    @pl.when(j <= i)
    def _():
        q = q_ref[...]
        kb = k_ref[...].astype(BF16)
        vb = v_ref[...].astype(BF16)
        causal = ((j * tq + lax.broadcasted_iota(jnp.int32, (tq, tq), 1))
                  <= (i * tq + lax.broadcasted_iota(jnp.int32, (tq, tq), 0)))
        acc = acc_ref[...]
        for h in range(N_HEADS):
            qh = jnp.where(lane_head == h, q, 0.0)
            s = _dot_nt(qh, kb) * (HEAD_DIM ** -0.5)
            s = s + (cc_ref[:, MISC_FOX_LANE + h:MISC_FOX_LANE + h + 1] - cr_ref[h:h + 1, :])
            s = jnp.where(causal, s, -1e30)
            m_prev = m_ref[h]
            m_new = jnp.maximum(m_prev, jnp.max(s, axis=-1, keepdims=True))
            alpha = jnp.exp(m_prev - m_new)
            p = jnp.exp(s - m_new[:, 0:1])
            l_ref[h] = alpha * l_ref[h] + jnp.sum(p, axis=-1, keepdims=True)
            m_ref[h] = m_new
            pv = jnp.dot(p.astype(BF16), vb, preferred_element_type=F32)
            acc = jnp.where(lane_head == h, acc * alpha[:, 0:1] + pv, acc)
        acc_ref[...] = acc

    @pl.when(j == i)
    def _():
        acc = acc_ref[...]
        out = jnp.zeros_like(acc)
        for h in range(N_HEADS):
            out = jnp.where(lane_head == h, acc / l_ref[h][:, 0:1], out)
        o_ref[...] = out


def _fox(z, c_col, c_row, batch, seq):
    t = z.shape[0]
    tiles = seq // ROW_TILE
    kv = lambda c: pl.BlockSpec((ROW_TILE, GROUP),
                                lambda b, i, j, c=c: (b * tiles + jnp.minimum(j, i), c))
    return pl.pallas_call(
        _fox_kernel,
        grid=(batch, tiles, tiles),
        in_specs=[pl.BlockSpec((ROW_TILE, GROUP), lambda b, i, j: (b * tiles + i, 9)),
                  kv(10), kv(11),
                  pl.BlockSpec((ROW_TILE, 128), lambda b, i, j: (b * tiles + i, 0)),
                  pl.BlockSpec((None, 8, ROW_TILE), lambda b, i, j: (b, 0, jnp.minimum(j, i)))],
        out_specs=pl.BlockSpec((ROW_TILE, GROUP), lambda b, i, j: (b * tiles + i, 0)),
        out_shape=jax.ShapeDtypeStruct((t, GROUP), F32),
        scratch_shapes=[pltpu.VMEM((N_HEADS, ROW_TILE, 128), F32),
                        pltpu.VMEM((N_HEADS, ROW_TILE, 128), F32),
                        pltpu.VMEM((ROW_TILE, GROUP), F32)],
        compiler_params=_cparams("parallel", "parallel", "arbitrary"),
        name="fox_attention",
    )(z, z, z, c_col, c_row)


def _outproj_kernel(ya_ref, yb_ref, yc_ref, yd_ref, w_ref, x_ref, g_ref, b_ref, o_ref):
    mix = _dot(ya_ref[...], w_ref[0]) + _dot(yb_ref[...], w_ref[1])
    mix = mix + (_dot(yc_ref[...], w_ref[2]) + _dot(yd_ref[...], w_ref[3]))
    o_ref[...] = _layernorm(DN_ALPHA * x_ref[...] + mix, g_ref[...], b_ref[...])


def _outproj(ya, yb, yc, yd, w_out, x, g, b, layer):
    t = x.shape[0]
    y_spec = pl.BlockSpec((ROW_TILE, GROUP), lambda i: (i, 0))
    vec = pl.BlockSpec((None, 1, D_MODEL), lambda i: (layer, 0, 0))
    return pl.pallas_call(
        _outproj_kernel,
        grid=(t // ROW_TILE,),
        in_specs=[y_spec, y_spec, y_spec, y_spec,
                  pl.BlockSpec((None, 4, GROUP, D_MODEL), lambda i: (layer, 0, 0, 0)),
                  pl.BlockSpec((ROW_TILE, D_MODEL), lambda i: (i, 0)), vec, vec],
        out_specs=pl.BlockSpec((ROW_TILE, D_MODEL), lambda i: (i, 0)),
        out_shape=jax.ShapeDtypeStruct((t, D_MODEL), F32),
        compiler_params=_cparams("parallel"),
        name="outproj_ln",
    )(ya, yb, yc, yd, w_out, x, g, b)


def _memkv_kernel(mem_ref, g_ref, b_ref, w_ref, o_ref):
    memn = _layernorm(mem_ref[...], g_ref[...], b_ref[...])
    o_ref[...] = _dot(memn, w_ref[...]).astype(BF16)


def _memkv(mem, g, b, w_kv):
    rows = mem.shape[0]
    n = w_kv.shape[0]
    return pl.pallas_call(
        _memkv_kernel,
        grid=(n,),
        in_specs=[pl.BlockSpec((rows, D_MODEL), lambda i: (0, 0)),
                  pl.BlockSpec((1, D_MODEL), lambda i: (0, 0)),
                  pl.BlockSpec((1, D_MODEL), lambda i: (0, 0)),
                  pl.BlockSpec((None, D_MODEL, D_MODEL), lambda i: (i, 0, 0))],
        out_specs=pl.BlockSpec((None, rows, D_MODEL), lambda i: (i, 0, 0)),
        out_shape=jax.ShapeDtypeStruct((n, rows, D_MODEL), BF16),
        compiler_params=_cparams("parallel"),
        name="memory_kv",
    )(mem, g, b, w_kv)


def _cross_kernel(x_ref, k_ref, v_ref, wq_ref, wo_ref, g_ref, b_ref, o_ref):
    x = x_ref[...]
    q = jnp.dot(x.astype(BF16), wq_ref[...], preferred_element_type=F32).astype(BF16)
    ca = None
    for h in range(CA_HEADS):
        sl = slice(h * CA_HEAD_DIM, (h + 1) * CA_HEAD_DIM)
        s = _dot_nt(q[:, sl], k_ref[:, sl]) * (CA_HEAD_DIM ** -0.5)
        s = s - jnp.max(s, axis=-1, keepdims=True)
        p = jnp.exp(s)
        p = p / jnp.sum(p, axis=-1, keepdims=True)
        oh = jnp.dot(p.astype(BF16), v_ref[:, sl], preferred_element_type=F32)
        part = jnp.dot(oh.astype(BF16), wo_ref[sl, :], preferred_element_type=F32)
        ca = part if ca is None else ca + part
    o_ref[...] = _layernorm(DN_ALPHA * x + ca, g_ref[...], b_ref[...])


def _cross_attention(x, kv, wq, wo, g, b, layer, batch, seq):
    t = x.shape[0]
    tiles = seq // ROW_TILE
    vec = pl.BlockSpec((None, 1, D_MODEL), lambda bb, i: (layer, 0, 0))
    mat = pl.BlockSpec((None, D_MODEL, D_MODEL), lambda bb, i: (layer, 0, 0))
    return pl.pallas_call(
        _cross_kernel,
        grid=(batch, tiles),
        in_specs=[pl.BlockSpec((ROW_TILE, D_MODEL), lambda bb, i: (bb * tiles + i, 0)),
                  pl.BlockSpec((None, N_MEM, D_MODEL), lambda bb, i: (2 * layer, bb, 0)),
                  pl.BlockSpec((None, N_MEM, D_MODEL), lambda bb, i: (2 * layer + 1, bb, 0)),
                  mat, mat, vec, vec],
        out_specs=pl.BlockSpec((ROW_TILE, D_MODEL), lambda bb, i: (bb * tiles + i, 0)),
        out_shape=jax.ShapeDtypeStruct((t, D_MODEL), F32),
        compiler_params=_cparams("parallel", "parallel"),
        name="cross_attention",
    )(x, kv, kv, wq, wo, g, b)


def _ffn_kernel(x_ref, wg_ref, wv_ref, bg_ref, bv_ref, cg_ref, cv_ref, cbg_ref, cbv_ref, wd_ref,
                g_ref, b_ref, o_ref, xb_ref, acc_ref, hg_ref, hv_ref, carry_ref, *, tiles_per_seq):
    i = pl.program_id(0)
    c = pl.program_id(1)
    tm = ROW_TILE

    @pl.when(c == 0)
    def _():
        xb_ref[...] = x_ref[...].astype(BF16)
        acc_ref[...] = jnp.zeros_like(acc_ref)

    first = (i % tiles_per_seq) == 0

    @pl.when(first)
    def _():
        carry_ref[0, c] = jnp.zeros((8, FFN_COL_TILE), F32)
        carry_ref[1, c] = jnp.zeros((8, FFN_COL_TILE), F32)

    def conv(h_ref, w_ref, bias_ref, cw_ref, cb_ref, slot):
        h = jnp.dot(xb_ref[...], w_ref[...], preferred_element_type=F32) + bias_ref[...]
        h_ref[0:8, :] = carry_ref[slot, c]
        h_ref[8:8 + tm, :] = h
        carry_ref[slot, c] = h[tm - 8:tm, :]
        cw = cw_ref[...]
        return (cw[0:1, :] * h_ref[6:6 + tm, :] + cw[1:2, :] * h_ref[7:7 + tm, :]
                + cw[2:3, :] * h + cb_ref[...])

    gate = conv(hg_ref, wg_ref, bg_ref, cg_ref, cbg_ref, 0)
    val = conv(hv_ref, wv_ref, bv_ref, cv_ref, cbv_ref, 1)
    act = 0.5 * gate * (1.0 + lax.erf(gate * (2.0 ** -0.5))) * val
    acc_ref[...] += jnp.dot(act.astype(BF16), wd_ref[...], preferred_element_type=F32)

    @pl.when(c == pl.num_programs(1) - 1)
    def _():
        o_ref[...] = _layernorm(DN_ALPHA * x_ref[...] + acc_ref[...], g_ref[...], b_ref[...])


def _ffn(x, p, layer, seq):
    t = x.shape[0]
    tf = FFN_COL_TILE
    nc = D_FF // tf
    up = lambda off: pl.BlockSpec((None, D_MODEL, tf), lambda i, c, off=off: (layer, 0, c + off))
    hvec = lambda off: pl.BlockSpec((None, 1, tf), lambda i, c, off=off: (layer, 0, c + off))
    cw = lambda off: pl.BlockSpec((None, 3, tf), lambda i, c, off=off: (layer, 0, c + off))
    vec = pl.BlockSpec((None, 1, D_MODEL), lambda i, c: (layer, 0, 0))
    return pl.pallas_call(
        functools.partial(_ffn_kernel, tiles_per_seq=seq // ROW_TILE),
        grid=(t // ROW_TILE, nc),
        in_specs=[pl.BlockSpec((ROW_TILE, D_MODEL), lambda i, c: (i, 0)),
                  up(0), up(nc), hvec(0), hvec(nc), cw(0), cw(nc), hvec(0), hvec(nc),
                  pl.BlockSpec((None, tf, D_MODEL), lambda i, c: (layer, c, 0)),
                  vec, vec],
        out_specs=pl.BlockSpec((ROW_TILE, D_MODEL), lambda i, c: (i, 0)),
        out_shape=jax.ShapeDtypeStruct((t, D_MODEL), F32),
        scratch_shapes=[pltpu.VMEM((ROW_TILE, D_MODEL), BF16), pltpu.VMEM((ROW_TILE, D_MODEL), F32),
                        pltpu.VMEM((ROW_TILE + 8, tf), F32), pltpu.VMEM((ROW_TILE + 8, tf), F32),
                        pltpu.VMEM((2, nc, 8, tf), F32)],
        compiler_params=_cparams("arbitrary", "arbitrary"),
        name="conv_ffn_ln",
    )(x, p["ffn_up"], p["ffn_up"], p["ffn_up_b"], p["ffn_up_b"], p["ffn_conv"], p["ffn_conv"],
      p["ffn_conv_b"], p["ffn_conv_b"], p["ffn_down"], p["ln3_g"], p["ln3_b"])


def _pack_params(w_in, w_out, sg_ln_g, sg_ln_b, sg_w, sg_b, rw_mu, rw_w0, rw_w2, rw_a0, rw_a2, rw_g2,
                 rw_kk, rw_ka, rw_rk, rw_lnx_g, rw_lnx_b, gla_a_up, gla_a_b, gla_norm_g, fox_fb,
                 ln1_g, ln1_b, ca_wq, ca_wk, ca_wv, ca_wo, ln2_g, ln2_b,
                 ffn_up, ffn_up_b, ffn_conv, ffn_conv_b, ffn_down, ln3_g, ln3_b):
    depth = w_in.shape[0]
    row = lambda a: a.reshape(depth, 1, -1)
    gla0, fox0 = 1536, 2320
    misc = jnp.zeros((depth, D_MODEL, 128), F32)
    misc = misc.at[:, :, 0:16].set(w_in[:, :, gla0 + 768:gla0 + 784])
    misc = misc.at[:, :, MISC_FOX_LANE:MISC_FOX_LANE + 4].set(w_in[:, :, fox0 + 768:fox0 + 772])
    w_pack = jnp.concatenate([w_in[:, :, 0:gla0 + 768], w_in[:, :, fox0:fox0 + 768], misc], axis=-1)

    def pad_rows(w, start):
        out = jnp.zeros((depth, GROUP, GROUP), F32)
        return out.at[:, start:start + w.shape[1], :].set(w).astype(BF16)

    fb = jnp.zeros((depth, 1, 128), F32).at[:, 0, MISC_FOX_LANE:MISC_FOX_LANE + 4].set(fox_fb)
    aup = jnp.zeros((depth, 128, N_HEADS * GLA_DK), F32).at[:, 0:16, :].set(gla_a_up)
    return {
        "w_in": w_pack.astype(BF16),
        "w_out": w_out.reshape(depth, 4, GROUP, D_MODEL).astype(BF16),
        "sg_ln_g": row(sg_ln_g), "sg_ln_b": row(sg_ln_b), "sg_w": sg_w,
        "sg_bias": jnp.repeat(jnp.swapaxes(sg_b, 1, 2), HEAD_DIM, axis=2),
        "rw_mu": rw_mu.reshape(depth, 4, 1, GROUP),
        "rw_w0": row(rw_w0), "rw_a0": row(rw_a0),
        "rw_w2": pad_rows(rw_w2, 0), "rw_a2": pad_rows(rw_a2, 64), "rw_g2": pad_rows(rw_g2, 128),
        "rw_kk": row(rw_kk), "rw_ka": row(rw_ka), "rw_rk": row(rw_rk),
        "rw_lnx_g": row(rw_lnx_g), "rw_lnx_b": row(rw_lnx_b),
        "gla_a_up": aup.astype(BF16), "gla_a_b": row(gla_a_b), "gla_norm_g": row(gla_norm_g),
        "fox_fb": fb,
        "ln1_g": row(ln1_g), "ln1_b": row(ln1_b),
        "ca_wq": ca_wq.astype(BF16), "ca_wo": ca_wo.astype(BF16),
        "ca_wkv": jnp.stack([ca_wk, ca_wv], axis=1).reshape(2 * depth, D_MODEL, D_MODEL).astype(BF16),
        "ln2_g": row(ln2_g), "ln2_b": row(ln2_b),
        "ffn_up": ffn_up.astype(BF16), "ffn_up_b": row(ffn_up_b), "ffn_conv": ffn_conv,
        "ffn_conv_b": row(ffn_conv_b), "ffn_down": ffn_down.astype(BF16),
        "ln3_g": row(ln3_g), "ln3_b": row(ln3_b),
    }


def _mixer_layer(x, p, layer, batch, seq):
    z = _inproj(x, p["w_in"], layer)
    y_a = _spatial_gating(z, p["sg_ln_g"], p["sg_ln_b"], p["sg_w"], p["sg_bias"], layer)
    y_b = _rwkv(z, p, layer, batch, seq)
    y_c = _gla(z, p, layer, batch, seq)
    c_col = _fox_gate(z, p["fox_fb"], layer, batch, seq)
    c_row = jnp.swapaxes(c_col[:, MISC_FOX_LANE:MISC_FOX_LANE + 4].reshape(batch, seq, 4), 1, 2)
    c_row = jnp.pad(c_row, ((0, 0), (0, 4), (0, 0)))
    y_d = _fox(z, c_col, c_row, batch, seq)
    return _outproj(y_a, y_b, y_c, y_d, p["w_out"], x, p["ln1_g"], p["ln1_b"], layer)


def kernel(x, mem, mem_ln_g, mem_ln_b, w_in, w_out, sg_ln_g, sg_ln_b, sg_w, sg_b, rw_mu, rw_w0, rw_w2, rw_a0, rw_a2, rw_g2, rw_kk, rw_ka, rw_rk, rw_lnx_g, rw_lnx_b, gla_a_up, gla_a_b, gla_norm_g, fox_fb, ln1_g, ln1_b, ca_wq, ca_wk, ca_wv, ca_wo, ln2_g, ln2_b, ffn_up, ffn_up_b, ffn_conv, ffn_conv_b, ffn_down, ln3_g, ln3_b):
    batch, seq, _ = x.shape
    depth = w_in.shape[0]
    p = _pack_params(w_in, w_out, sg_ln_g, sg_ln_b, sg_w, sg_b, rw_mu, rw_w0, rw_w2, rw_a0, rw_a2,
                     rw_g2, rw_kk, rw_ka, rw_rk, rw_lnx_g, rw_lnx_b, gla_a_up, gla_a_b, gla_norm_g,
                     fox_fb, ln1_g, ln1_b, ca_wq, ca_wk, ca_wv, ca_wo, ln2_g, ln2_b,
                     ffn_up, ffn_up_b, ffn_conv, ffn_conv_b, ffn_down, ln3_g, ln3_b)
    kv = _memkv(mem.reshape(batch * N_MEM, D_MODEL), mem_ln_g.reshape(1, -1), mem_ln_b.reshape(1, -1),
                p["ca_wkv"])
    h = x.reshape(batch * seq, D_MODEL)
    for layer in range(depth):
        h = _mixer_layer(h, p, layer, batch, seq)
        h = _cross_attention(h, kv, p["ca_wq"], p["ca_wo"], p["ln2_g"], p["ln2_b"], layer, batch, seq)
        h = _ffn(h, p, layer, seq)
    return h.reshape(batch, seq, D_MODEL)
```

```python
import functools

import jax
import jax.numpy as jnp
from jax import lax
from jax.experimental import pallas as pl
from jax.experimental.pallas import tpu as pltpu

F32 = jnp.float32
BF16 = jnp.bfloat16

D_MODEL = 1024
DEPTH = 4
N_MEM = 256
HEAD_DIM = 64
N_HEADS = 4
GROUP = 256
SG_CHUNK = 128
RW_LNX_EPS = 64e-5
RW_CHUNK = 64
GLA_DK = 32
GLA_TAU = 16.0
GLA_CHUNK = 64
GLA_NORM_EPS = 1e-5
CA_HEADS = 4
CA_HEAD_DIM = 256
D_FF = 2816
DN_ALPHA = (2.0 * DEPTH) ** 0.25
LN_EPS = 1e-5

P_PACK = 12 * 256 + 128
MISC_BLOCK = 24
MISC_FOX_LANE = 16

ROW_TILE = 512
FFN_COL_TILE = 256
VMEM_LIMIT = 48 * 1024 * 1024


def _cparams(*sem):
    return pltpu.CompilerParams(dimension_semantics=sem, vmem_limit_bytes=VMEM_LIMIT)


def _dot(a, b):
    return jnp.dot(a.astype(BF16), b.astype(BF16), preferred_element_type=F32)


def _dot_nt(a, b):
    return lax.dot_general(a.astype(BF16), b.astype(BF16), (((1,), (1,)), ((), ())),
                           preferred_element_type=F32)


def _dot_tn(a, b):
    return lax.dot_general(a.astype(BF16), b.astype(BF16), (((0,), (0,)), ((), ())),
                           preferred_element_type=F32)


def _split2(a):
    hi = a.astype(BF16)
    lo = (a - hi.astype(F32)).astype(BF16)
    return hi, lo


def _dot3(a, b):
    ah, al = _split2(a)
    bh, bl = _split2(b)
    return (jnp.dot(ah, bh, preferred_element_type=F32)
            + (jnp.dot(ah, bl, preferred_element_type=F32) + jnp.dot(al, bh, preferred_element_type=F32)))


def _group_ones(n, shift):
    r = lax.broadcasted_iota(jnp.int32, (n, n), 0) >> shift
    c = lax.broadcasted_iota(jnp.int32, (n, n), 1) >> shift
    return jnp.where(r == c, 1.0, 0.0).astype(BF16)


def _gsum(x, ones):
    hi, lo = _split2(x)
    return jnp.dot(hi, ones, preferred_element_type=F32) + jnp.dot(lo, ones, preferred_element_type=F32)


def _lane_head(width, shift):
    return lax.broadcasted_iota(jnp.int32, (1, width), 1) >> shift


def _softplus(x):
    return jnp.maximum(x, 0.0) + jnp.log1p(jnp.exp(-jnp.abs(x)))


def _log_sigmoid(x):
    return jnp.minimum(x, 0.0) - jnp.log1p(jnp.exp(-jnp.abs(x)))


def _sigmoid(x):
    return 1.0 / (1.0 + jnp.exp(-x))


def _layernorm(x, g, b):
    mu = jnp.mean(x, axis=-1, keepdims=True)
    d = x - mu
    var = jnp.mean(d * d, axis=-1, keepdims=True)
    return d * lax.rsqrt(var + LN_EPS) * g + b


def _seg_cumsum(x, seg):
    pos = lax.broadcasted_iota(jnp.int32, x.shape, 0) & (seg - 1)
    k = 1
    while k < seg:
        x = x + jnp.where(pos >= k, pltpu.roll(x, k, 0), 0.0)
        k *= 2
    return x


def _stack_heads(x, lane_head):
    return jnp.concatenate([jnp.where(lane_head == h, x, 0.0) for h in range(N_HEADS)], axis=0)


def _unstack_heads(x, c):
    return (x[0:c] + x[c:2 * c]) + (x[2 * c:3 * c] + x[3 * c:4 * c])


def _inproj_kernel(x_ref, w_ref, o_ref):
    o_ref[...] = jnp.dot(x_ref[...].astype(BF16), w_ref[...], preferred_element_type=F32)


def _inproj(x, w_pack, layer):
    t = x.shape[0]
    return pl.pallas_call(
        _inproj_kernel,
        grid=(t // ROW_TILE,),
        in_specs=[pl.BlockSpec((ROW_TILE, D_MODEL), lambda i: (i, 0)),
                  pl.BlockSpec((None, D_MODEL, P_PACK), lambda i: (layer, 0, 0))],
        out_specs=pl.BlockSpec((ROW_TILE, P_PACK), lambda i: (i, 0)),
        out_shape=jax.ShapeDtypeStruct((t, P_PACK), F32),
        compiler_params=_cparams("parallel"),
        name="inproj",
    )(x, w_pack)


def _sg_kernel(u_ref, v_ref, g_ref, b_ref, w_ref, sb_ref, o_ref):
    ones = _group_ones(GROUP, 6)
    lane_head = _lane_head(GROUP, 6)
    v = v_ref[...]
    mean = _gsum(v, ones) * (1.0 / HEAD_DIM)
    d = v - mean
    var = _gsum(d * d, ones) * (1.0 / HEAD_DIM)
    vn = d * lax.rsqrt(var + LN_EPS) * g_ref[...] + b_ref[...]
    row = lax.broadcasted_iota(jnp.int32, (SG_CHUNK, SG_CHUNK), 0)
    col = lax.broadcasted_iota(jnp.int32, (SG_CHUNK, SG_CHUNK), 1)
    wm = [jnp.where(col <= row, w_ref[h], 0.0).astype(BF16) for h in range(N_HEADS)]
    for c in range(ROW_TILE // SG_CHUNK):
        sl = slice(c * SG_CHUNK, (c + 1) * SG_CHUNK)
        vc = vn[sl]
        s = sb_ref[...]
        for h in range(N_HEADS):
            s = s + jnp.dot(wm[h], jnp.where(lane_head == h, vc, 0.0).astype(BF16),
                            preferred_element_type=F32)
        o_ref[sl, :] = u_ref[sl, :] * s


def _spatial_gating(z, ln_g, ln_b, w_s, sbias, layer):
    t = z.shape[0]
    return pl.pallas_call(
        _sg_kernel,
        grid=(t // ROW_TILE,),
        in_specs=[pl.BlockSpec((ROW_TILE, GROUP), lambda i: (i, 0)),
                  pl.BlockSpec((ROW_TILE, GROUP), lambda i: (i, 1)),
                  pl.BlockSpec((None, 1, GROUP), lambda i: (layer, 0, 0)),
                  pl.BlockSpec((None, 1, GROUP), lambda i: (layer, 0, 0)),
                  pl.BlockSpec((None, N_HEADS, SG_CHUNK, SG_CHUNK), lambda i: (layer, 0, 0, 0)),
                  pl.BlockSpec((None, SG_CHUNK, GROUP), lambda i: (layer, 0, 0))],
        out_specs=pl.BlockSpec((ROW_TILE, GROUP), lambda i: (i, 0)),
        out_shape=jax.ShapeDtypeStruct((t, GROUP), F32),
        compiler_params=_cparams("parallel"),
        name="spatial_gating",
    )(z, z, ln_g, ln_b, w_s, sbias)


def _tri_inverse(mats, same_head, rr, cc):
    n = mats[0].shape[0]
    eye = jnp.where(lax.broadcasted_iota(jnp.int32, (n, n), 0)
                    == lax.broadcasted_iota(jnp.int32, (n, n), 1), 1.0, 0.0)
    b16 = (rr >> 4) == (cc >> 4)
    b32 = (rr >> 5) == (cc >> 5)
    in16 = same_head & b16
    in32 = same_head & b32 & jnp.logical_not(b16)
    in64 = same_head & jnp.logical_not(b32)
    x = [jnp.where(in16, a, 0.0) for a in mats]
    t = [eye + d for d in x]
    for _ in range(3):
        x = [_dot(d, d) for d in x]
        t = [ti + _dot(ti, d) for ti, d in zip(t, x)]
    for level in (in32, in64):
        te = [_dot(ti, jnp.where(level, a, 0.0)) for ti, a in zip(t, mats)]
        t = [ti + _dot(tei, ti) for ti, tei in zip(t, te)]
    return t


def _rwkv_kernel(r_ref, k_ref, v_ref, lo_ref, mu_ref, w0_ref, a0_ref, w2_ref, a2_ref, g2_ref,
                 kk_ref, ka_ref, rk_ref, lng_ref, lnb_ref, o_ref,
                 prev_ref, st_ref, at_s, bt_s, kt_s, rt_s, v_s, wc_s, y_s, ta_s, uv_s, rf_s):
    c_len = RW_CHUNK
    n_chunks = ROW_TILE // c_len

    @pl.when(pl.program_id(1) == 0)
    def _():
        prev_ref[...] = jnp.zeros_like(prev_ref)
        st_ref[...] = jnp.zeros_like(st_ref)

    ones = _group_ones(GROUP, 6)
    lane_head = _lane_head(GROUP, 6)
    row0 = lax.broadcasted_iota(jnp.int32, (ROW_TILE, GROUP), 0) == 0

    def shifted(ref, j):
        z = ref[...]
        prev = prev_ref[j]
        zs = jnp.where(row0, prev[7:8, :], pltpu.roll(z, 1, 0))
        prev_ref[j] = z[ROW_TILE - 8:ROW_TILE, :]
        return z + mu_ref[j] * (zs - z)

    r = shifted(r_ref, 0)
    k = shifted(k_ref, 1)
    v = shifted(v_ref, 2)
    lo = shifted(lo_ref, 3)

    wlog = -_softplus(-(w0_ref[...] + _dot(jnp.tanh(lo), w2_ref[...]))) - 0.5
    ld = -jnp.exp(wlog)
    a = _sigmoid(a0_ref[...] + _dot(lo, a2_ref[...]))
    gate = _dot(_sigmoid(lo), g2_ref[...])
    kk = k * kk_ref[...]
    kk = kk / jnp.maximum(jnp.sqrt(_gsum(kk * kk, ones)), 1e-12)
    k = k * (1.0 + (a - 1.0) * ka_ref[...])
    bonus = _gsum(r * k * rk_ref[...], ones) * v

    cl = _seg_cumsum(ld, c_len)
    e_pos = jnp.exp(cl)
    e_neg = jnp.exp(-cl)
    at_s[...] = -kk * jnp.exp(cl - ld)
    bt_s[...] = kk * a * e_neg
    kt_s[...] = k * e_neg
    rt_s[...] = r * e_pos
    v_s[...] = v
    wc_s[...] = e_pos

    n4 = N_HEADS * c_len
    ri = lax.broadcasted_iota(jnp.int32, (n4, n4), 0)
    ci = lax.broadcasted_iota(jnp.int32, (n4, n4), 1)
    same_head = (ri >> 6) == (ci >> 6)
    rr = ri & (c_len - 1)
    cc = ci & (c_len - 1)
    wt = lax.broadcasted_iota(jnp.int32, (c_len, n4), 0)
    ws = lax.broadcasted_iota(jnp.int32, (c_len, n4), 1) & (c_len - 1)
    strict_w = ws < wt
    incl_w = ws <= wt

    chunks = range(n_chunks)
    rows = [slice(c * c_len, (c + 1) * c_len) for c in chunks]
    gram = [_dot_nt(jnp.concatenate([at_s[sl, :], rt_s[sl, :]], axis=0),
                    jnp.concatenate([_stack_heads(bt_s[sl, :], lane_head),
                                     _stack_heads(kt_s[sl, :], lane_head)], axis=0)) for sl in rows]
    a_ab = [jnp.where(strict_w, g[0:c_len, 0:n4], 0.0) for g in gram]
    t_full = _tri_inverse(
        [jnp.where(same_head, jnp.concatenate([a] * N_HEADS, axis=0), 0.0) for a in a_ab],
        same_head, rr, cc)
    for c, sl in enumerate(rows):
        g = gram[c]
        a_ak = jnp.where(strict_w, g[0:c_len, n4:2 * n4], 0.0)
        l_rb = jnp.where(incl_w, g[c_len:2 * c_len, 0:n4], 0.0)
        l_rk = jnp.where(incl_w, g[c_len:2 * c_len, n4:2 * n4], 0.0)
        t_w = _unstack_heads(t_full[c], c_len)
        v_stack = _stack_heads(v_s[sl, :], lane_head)
        ta = _dot(t_w, _stack_heads(at_s[sl, :], lane_head))
        uv = _dot(t_w, _stack_heads(_dot(a_ak, v_stack), lane_head))
        ta_s[sl, :] = ta
        uv_s[sl, :] = uv
        rf_s[sl, :] = rt_s[sl, :] + _dot(l_rb, _stack_heads(ta, lane_head))
        y_s[sl, :] = _dot(l_rb, _stack_heads(uv, lane_head)) + _dot(l_rk, v_stack)

    def inter(c, carry):
        sl = pl.ds(pl.multiple_of(c * c_len, c_len), c_len)
        st = st_ref[...]
        x = _dot_nt(jnp.concatenate([ta_s[sl, :], rf_s[sl, :]], axis=0), st)
        u = x[0:c_len] + uv_s[sl, :]
        y_s[sl, :] = y_s[sl, :] + x[c_len:2 * c_len]
        upd = _dot_tn(jnp.concatenate([u, v_s[sl, :]], axis=0),
                      jnp.concatenate([bt_s[sl, :], kt_s[sl, :]], axis=0))
        wc = wc_s[pl.ds(pl.multiple_of(c * c_len + (c_len - 8), 8), 8), :]
        st_ref[...] = (st + jnp.where(same_head, upd, 0.0)) * wc[7:8, :]
        return carry

    lax.fori_loop(0, n_chunks, inter, 0)

    y = y_s[...]
    mean = _gsum(y, ones) * (1.0 / HEAD_DIM)
    d = y - mean
    var = _gsum(d * d, ones) * (1.0 / HEAD_DIM)
    y = d * lax.rsqrt(var + RW_LNX_EPS) * lng_ref[...] + lnb_ref[...]
    o_ref[...] = (y + bonus) * gate


def _rwkv(z, p, layer, batch, seq):
    t = z.shape[0]
    tiles = seq // ROW_TILE
    row = lambda c: pl.BlockSpec((ROW_TILE, GROUP), lambda b, i, c=c: (b * tiles + i, c))
    vec = pl.BlockSpec((None, 1, GROUP), lambda b, i: (layer, 0, 0))
    mat = pl.BlockSpec((None, GROUP, GROUP), lambda b, i: (layer, 0, 0))
    scr = pltpu.VMEM((ROW_TILE, GROUP), F32)
    return pl.pallas_call(
        _rwkv_kernel,
        grid=(batch, tiles),
        in_specs=[row(2), row(3), row(4), row(5),
                  pl.BlockSpec((None, 4, 1, GROUP), lambda b, i: (layer, 0, 0, 0)),
                  vec, vec, mat, mat, mat, vec, vec, vec, vec, vec],
        out_specs=pl.BlockSpec((ROW_TILE, GROUP), lambda b, i: (b * tiles + i, 0)),
        out_shape=jax.ShapeDtypeStruct((t, GROUP), F32),
        scratch_shapes=[pltpu.VMEM((4, 8, GROUP), F32), pltpu.VMEM((GROUP, GROUP), F32),
                        scr, scr, scr, scr, scr, scr, scr, scr, scr, scr],
        compiler_params=_cparams("parallel", "arbitrary"),
        name="rwkv7",
    )(z, z, z, z, p["rw_mu"], p["rw_w0"], p["rw_a0"], p["rw_w2"], p["rw_a2"], p["rw_g2"],
      p["rw_kk"], p["rw_ka"], p["rw_rk"], p["rw_lnx_g"], p["rw_lnx_b"])


def _gla_kernel(qk_ref, v_ref, g_ref, m_ref, aup_ref, ab_ref, ng_ref, o_ref,
                st_ref, qe_s, ke_s, k2_s, qi_s, dec_s, o_s):
    c_len = GLA_CHUNK
    n_chunks = ROW_TILE // c_len
    kw = N_HEADS * GLA_DK

    @pl.when(pl.program_id(1) == 0)
    def _():
        st_ref[...] = jnp.zeros_like(st_ref)

    lg = _log_sigmoid(_dot(m_ref[...], aup_ref[...]) + ab_ref[...]) * (1.0 / GLA_TAU)
    b = _seg_cumsum(lg, c_len).reshape(n_chunks, c_len, kw)
    b_mid = b[:, c_len // 2:c_len // 2 + 1, :]
    b_last = b[:, c_len - 1:c_len, :]
    q = (qk_ref[:, 0:kw] * (GLA_DK ** -0.5)).reshape(n_chunks, c_len, kw)
    k = qk_ref[:, kw:2 * kw].reshape(n_chunks, c_len, kw)
    qe_s[...] = (q * jnp.exp(b - b_mid)).reshape(ROW_TILE, kw)
    ke_s[...] = (k * jnp.exp(b_mid - b)).reshape(ROW_TILE, kw)
    k2_s[...] = (k * jnp.exp(b_last - b)).reshape(ROW_TILE, kw)
    qi_s[...] = (q * jnp.exp(b)).reshape(ROW_TILE, kw)
    dec_s[...] = jnp.exp(b).reshape(ROW_TILE, kw)

    k_head = _lane_head(kw, 5)
    v_head = _lane_head(GROUP, 6)
    causal = ((lax.broadcasted_iota(jnp.int32, (c_len, N_HEADS * c_len), 1) & (c_len - 1))
              <= lax.broadcasted_iota(jnp.int32, (c_len, N_HEADS * c_len), 0))
    same_head = ((lax.broadcasted_iota(jnp.int32, (GROUP, kw), 0) >> 6)
                 == (lax.broadcasted_iota(jnp.int32, (GROUP, kw), 1) >> 5))

    def chunk(c, carry):
        base = pl.multiple_of(c * c_len, c_len)
        sl = pl.ds(base, c_len)
        vc = v_ref[sl, :]
        st = st_ref[...]
        att = jnp.where(causal, _dot_nt(qe_s[sl, :], _stack_heads(ke_s[sl, :], k_head)), 0.0)
        o_s[sl, :] = _dot(att, _stack_heads(vc, v_head)) + _dot_nt(qi_s[sl, :], st)
        dec = dec_s[pl.ds(pl.multiple_of(base + (c_len - 8), 8), 8), :]
        st_ref[...] = st * dec[7:8, :] + jnp.where(same_head, _dot_tn(vc, k2_s[sl, :]), 0.0)
        return carry

    lax.fori_loop(0, n_chunks, chunk, 0)

    o = o_s[...]
    ms = _gsum(o * o, _group_ones(GROUP, 6)) * (1.0 / HEAD_DIM)
    g = g_ref[...]
    o_ref[...] = o * lax.rsqrt(ms + GLA_NORM_EPS) * ng_ref[...] * (g * _sigmoid(g))


def _gla(z, p, layer, batch, seq):
    t = z.shape[0]
    tiles = seq // ROW_TILE
    kw = N_HEADS * GLA_DK
    row = lambda c: pl.BlockSpec((ROW_TILE, GROUP), lambda b, i, c=c: (b * tiles + i, c))
    kscr = pltpu.VMEM((ROW_TILE, kw), F32)
    return pl.pallas_call(
        _gla_kernel,
        grid=(batch, tiles),
        in_specs=[row(6), row(7), row(8),
                  pl.BlockSpec((ROW_TILE, 128), lambda b, i: (b * tiles + i, MISC_BLOCK)),
                  pl.BlockSpec((None, 128, kw), lambda b, i: (layer, 0, 0)),
                  pl.BlockSpec((None, 1, kw), lambda b, i: (layer, 0, 0)),
                  pl.BlockSpec((None, 1, GROUP), lambda b, i: (layer, 0, 0))],
        out_specs=pl.BlockSpec((ROW_TILE, GROUP), lambda b, i: (b * tiles + i, 0)),
        out_shape=jax.ShapeDtypeStruct((t, GROUP), F32),
        scratch_shapes=[pltpu.VMEM((GROUP, kw), F32), kscr, kscr, kscr, kscr, kscr,
                        pltpu.VMEM((ROW_TILE, GROUP), F32)],
        compiler_params=_cparams("parallel", "arbitrary"),
        name="gla",
    )(z, z, z, z, p["gla_a_up"], p["gla_a_b"], p["gla_norm_g"])


def _fox_prep_kernel(q_ref, k_ref, v_ref, z_ref, fb_ref, qa_ref, ka_ref, vb_ref):
    c = _seg_cumsum(_log_sigmoid(z_ref[...] + fb_ref[...]), z_ref.shape[0])
    lane = lax.broadcasted_iota(jnp.int32, (1, 128), 1)
    for h in range(N_HEADS):
        ch = c[:, MISC_FOX_LANE + h:MISC_FOX_LANE + h + 1]
        c1 = ch.astype(BF16).astype(F32)
        c2 = (ch - c1).astype(BF16).astype(F32)
        c3 = ch - c1 - c2
        tile = slice((h // 2) * 128, (h // 2 + 1) * 128)
        off = (h % 2) * HEAD_DIM
        aug = HEAD_DIM - off
        in_head = (lane >> 6) == (h % 2)
        qa = jnp.where((lane >= aug + 3) & (lane < aug + 6), 1.0, 0.0)
        ka = jnp.where((lane >= aug) & (lane < aug + 3), 1.0, 0.0)
        for n, cn in enumerate((c1, c2, c3)):
            qa = jnp.where(lane == aug + n, cn, qa)
            ka = jnp.where(lane == aug + 3 + n, -cn, ka)
        qa = jnp.where(in_head, q_ref[:, tile] * (HEAD_DIM ** -0.5), qa)
        ka = jnp.where(in_head, k_ref[:, tile], ka)
        qa_ref[h] = qa.astype(BF16)
        ka_ref[h] = ka.astype(BF16)
    vb_ref[...] = v_ref[...].astype(BF16)


def _fox_prep(z, fb, layer, batch, seq):
    t = z.shape[0]
    col = lambda c: pl.BlockSpec((seq, GROUP), lambda b, c=c: (b, c))
    head = pl.BlockSpec((N_HEADS, seq, 128), lambda b: (0, b, 0))
    return pl.pallas_call(
        _fox_prep_kernel,
        grid=(batch,),
        in_specs=[col(9), col(10), col(11),
                  pl.BlockSpec((seq, 128), lambda b: (b, MISC_BLOCK)),
                  pl.BlockSpec((None, 1, 128), lambda b: (layer, 0, 0))],
        out_specs=[head, head, pl.BlockSpec((seq, GROUP), lambda b: (b, 0))],
        out_shape=[jax.ShapeDtypeStruct((N_HEADS, t, 128), BF16),
                   jax.ShapeDtypeStruct((N_HEADS, t, 128), BF16),
                   jax.ShapeDtypeStruct((t, GROUP), BF16)],
        compiler_params=_cparams("parallel"),
        name="fox_prep",
    )(z, z, z, z, fb)


def _fox_kernel(qa_ref, ka_ref, v_ref, o_ref, m_ref, l_ref, acc_ref):
    tq = ROW_TILE
    i = pl.program_id(1)
    j = pl.program_id(2)

    @pl.when(j == 0)
    def _():
        m_ref[...] = jnp.full_like(m_ref, -1e30)
        l_ref[...] = jnp.zeros_like(l_ref)
        acc_ref[...] = jnp.zeros_like(acc_ref)

    lane_head = _lane_head(GROUP, 6)

    def update(on_diagonal):
        vb = v_ref[...]
        acc = acc_ref[...]
        if on_diagonal:
            causal = (lax.broadcasted_iota(jnp.int32, (tq, tq), 1)
                      <= lax.broadcasted_iota(jnp.int32, (tq, tq), 0))
        heads = range(N_HEADS)
        s = [lax.dot_general(qa_ref[h], ka_ref[h], (((1,), (1,)), ((), ())),
                             preferred_element_type=F32) for h in heads]
        if on_diagonal:
            s = [jnp.where(causal, s[h], -1e30) for h in heads]
        m_prev = [m_ref[h] for h in heads]
        m_new = [jnp.maximum(m_prev[h], jnp.max(s[h], axis=-1, keepdims=True)) for h in heads]
        alpha = [jnp.exp(m_prev[h] - m_new[h]) for h in heads]
        p = [jnp.exp(s[h] - m_new[h][:, 0:1]) for h in heads]
        for h in heads:
            l_ref[h] = alpha[h] * l_ref[h] + jnp.sum(p[h], axis=-1, keepdims=True)
            m_ref[h] = m_new[h]
        pv = [jnp.dot(p[h].astype(BF16), vb, preferred_element_type=F32) for h in heads]
        by_head = lambda xs: jnp.where(lane_head < 2, jnp.where(lane_head == 0, xs[0], xs[1]),
                                       jnp.where(lane_head == 2, xs[2], xs[3]))
        return acc * by_head([alpha[h][:, 0:1] for h in heads]) + by_head(pv)

    @pl.when(j < i)
    def _():
        acc_ref[...] = update(False)

    @pl.when(j == i)
    def _():
        acc = update(True)
        out = jnp.zeros_like(acc)
        for h in range(N_HEADS):
            out = jnp.where(lane_head == h, acc / l_ref[h][:, 0:1], out)
        o_ref[...] = out


def _fox(qa, ka, vb, batch, seq):
    t = vb.shape[0]
    tiles = seq // ROW_TILE
    return pl.pallas_call(
        _fox_kernel,
        grid=(batch, tiles, tiles),
        in_specs=[pl.BlockSpec((N_HEADS, ROW_TILE, 128), lambda b, i, j: (0, b * tiles + i, 0)),
                  pl.BlockSpec((N_HEADS, ROW_TILE, 128),
                               lambda b, i, j: (0, b * tiles + jnp.minimum(j, i), 0)),
                  pl.BlockSpec((ROW_TILE, GROUP), lambda b, i, j: (b * tiles + jnp.minimum(j, i), 0))],
        out_specs=pl.BlockSpec((ROW_TILE, GROUP), lambda b, i, j: (b * tiles + i, 0)),
        out_shape=jax.ShapeDtypeStruct((t, GROUP), F32),
        scratch_shapes=[pltpu.VMEM((N_HEADS, ROW_TILE, 128), F32),
                        pltpu.VMEM((N_HEADS, ROW_TILE, 128), F32),
                        pltpu.VMEM((ROW_TILE, GROUP), F32)],
        compiler_params=_cparams("parallel", "parallel", "arbitrary"),
        name="fox_attention",
    )(qa, ka, vb)


def _outproj_kernel(ya_ref, yb_ref, yc_ref, yd_ref, w_ref, x_ref, g_ref, b_ref, o_ref):
    mix = _dot(ya_ref[...], w_ref[0]) + _dot(yb_ref[...], w_ref[1])
    mix = mix + (_dot(yc_ref[...], w_ref[2]) + _dot(yd_ref[...], w_ref[3]))
    o_ref[...] = _layernorm(DN_ALPHA * x_ref[...] + mix, g_ref[...], b_ref[...])


def _outproj(ya, yb, yc, yd, w_out, x, g, b, layer):
    t = x.shape[0]
    y_spec = pl.BlockSpec((ROW_TILE, GROUP), lambda i: (i, 0))
    vec = pl.BlockSpec((None, 1, D_MODEL), lambda i: (layer, 0, 0))
    return pl.pallas_call(
        _outproj_kernel,
        grid=(t // ROW_TILE,),
        in_specs=[y_spec, y_spec, y_spec, y_spec,
                  pl.BlockSpec((None, 4, GROUP, D_MODEL), lambda i: (layer, 0, 0, 0)),
                  pl.BlockSpec((ROW_TILE, D_MODEL), lambda i: (i, 0)), vec, vec],
        out_specs=pl.BlockSpec((ROW_TILE, D_MODEL), lambda i: (i, 0)),
        out_shape=jax.ShapeDtypeStruct((t, D_MODEL), F32),
        compiler_params=_cparams("parallel"),
        name="outproj_ln",
    )(ya, yb, yc, yd, w_out, x, g, b)


def _memkv_kernel(mem_ref, g_ref, b_ref, w_ref, o_ref):
    memn = _layernorm(mem_ref[...], g_ref[...], b_ref[...])
    o_ref[...] = _dot(memn, w_ref[...]).astype(BF16)


def _memkv(mem, g, b, w_kv):
    rows = mem.shape[0]
    n = w_kv.shape[0]
    return pl.pallas_call(
        _memkv_kernel,
        grid=(n,),
        in_specs=[pl.BlockSpec((rows, D_MODEL), lambda i: (0, 0)),
                  pl.BlockSpec((1, D_MODEL), lambda i: (0, 0)),
                  pl.BlockSpec((1, D_MODEL), lambda i: (0, 0)),
                  pl.BlockSpec((None, D_MODEL, D_MODEL), lambda i: (i, 0, 0))],
        out_specs=pl.BlockSpec((None, rows, D_MODEL), lambda i: (i, 0, 0)),
        out_shape=jax.ShapeDtypeStruct((n, rows, D_MODEL), BF16),
        compiler_params=_cparams("parallel"),
        name="memory_kv",
    )(mem, g, b, w_kv)


def _cross_kernel(x_ref, k_ref, v_ref, wq_ref, wo_ref, g_ref, b_ref, o_ref):
    x = x_ref[...]
    q = jnp.dot(x.astype(BF16), wq_ref[...], preferred_element_type=F32).astype(BF16)
    ca = None
    for h in range(CA_HEADS):
        sl = slice(h * CA_HEAD_DIM, (h + 1) * CA_HEAD_DIM)
        s = _dot_nt(q[:, sl], k_ref[:, sl]) * (CA_HEAD_DIM ** -0.5)
        s = s - jnp.max(s, axis=-1, keepdims=True)
        p = jnp.exp(s)
        p = p / jnp.sum(p, axis=-1, keepdims=True)
        oh = jnp.dot(p.astype(BF16), v_ref[:, sl], preferred_element_type=F32)
        part = jnp.dot(oh.astype(BF16), wo_ref[sl, :], preferred_element_type=F32)
        ca = part if ca is None else ca + part
    o_ref[...] = _layernorm(DN_ALPHA * x + ca, g_ref[...], b_ref[...])


def _cross_attention(x, kv, wq, wo, g, b, layer, batch, seq):
    t = x.shape[0]
    tiles = seq // ROW_TILE
    vec = pl.BlockSpec((None, 1, D_MODEL), lambda bb, i: (layer, 0, 0))
    mat = pl.BlockSpec((None, D_MODEL, D_MODEL), lambda bb, i: (layer, 0, 0))
    return pl.pallas_call(
        _cross_kernel,
        grid=(batch, tiles),
        in_specs=[pl.BlockSpec((ROW_TILE, D_MODEL), lambda bb, i: (bb * tiles + i, 0)),
                  pl.BlockSpec((None, N_MEM, D_MODEL), lambda bb, i: (2 * layer, bb, 0)),
                  pl.BlockSpec((None, N_MEM, D_MODEL), lambda bb, i: (2 * layer + 1, bb, 0)),
                  mat, mat, vec, vec],
        out_specs=pl.BlockSpec((ROW_TILE, D_MODEL), lambda bb, i: (bb * tiles + i, 0)),
        out_shape=jax.ShapeDtypeStruct((t, D_MODEL), F32),
        compiler_params=_cparams("parallel", "parallel"),
        name="cross_attention",
    )(x, kv, kv, wq, wo, g, b)


def _ffn_kernel(x_ref, wu_ref, bu_ref, cw_ref, cb_ref, wd_ref, g_ref, b_ref, o_ref,
                act_ref, carry_ref, *, tiles_per_seq):
    tm = ROW_TILE
    tf = FFN_COL_TILE
    first = (pl.program_id(0) % tiles_per_seq) == 0

    @pl.when(first)
    def _():
        carry_ref[...] = jnp.zeros_like(carry_ref)

    x = x_ref[...]
    xb = x.astype(BF16)
    row8 = lax.broadcasted_iota(jnp.int32, (8, tf), 0)

    def conv(col):
        cols = slice(col, col + tf)
        h = jnp.dot(xb, wu_ref[:, cols], preferred_element_type=F32) + bu_ref[:, cols]
        prev = carry_ref[:, cols]
        carry_ref[:, cols] = h[tm - 8:tm, :]
        h1 = pltpu.roll(h, 1, 0)
        h2 = pltpu.roll(h, 2, 0)
        top1 = jnp.where(row8 == 0, prev[7:8, :], h1[0:8, :])
        top2 = jnp.where(row8 == 0, prev[6:7, :], jnp.where(row8 == 1, prev[7:8, :], h2[0:8, :]))
        h1 = jnp.concatenate([top1, h1[8:tm, :]], axis=0)
        h2 = jnp.concatenate([top2, h2[8:tm, :]], axis=0)
        cw = cw_ref[:, cols]
        return cw[0:1, :] * h2 + cw[1:2, :] * h1 + cw[2:3, :] * h + cb_ref[:, cols]

    for c in range(D_FF // tf):
        gate = conv(c * tf)
        val = conv(D_FF + c * tf)
        act = 0.5 * gate * (1.0 + lax.erf(gate * (2.0 ** -0.5))) * val
        act_ref[:, c * tf:(c + 1) * tf] = act.astype(BF16)

    ff = jnp.dot(act_ref[...], wd_ref[...], preferred_element_type=F32)
    o_ref[...] = _layernorm(DN_ALPHA * x + ff, g_ref[...], b_ref[...])


def _ffn(x, p, layer, seq):
    t = x.shape[0]
    once = pl.Buffered(1)
    vec = pl.BlockSpec((None, 1, D_MODEL), lambda i: (layer, 0, 0))
    wide = lambda rows: pl.BlockSpec((None, rows, 2 * D_FF), lambda i: (layer, 0, 0))
    return pl.pallas_call(
        functools.partial(_ffn_kernel, tiles_per_seq=seq // ROW_TILE),
        grid=(t // ROW_TILE,),
        in_specs=[pl.BlockSpec((ROW_TILE, D_MODEL), lambda i: (i, 0)),
                  pl.BlockSpec((None, D_MODEL, 2 * D_FF), lambda i: (layer, 0, 0), pipeline_mode=once),
                  wide(1), wide(3), wide(1),
                  pl.BlockSpec((None, D_FF, D_MODEL), lambda i: (layer, 0, 0), pipeline_mode=once),
                  vec, vec],
        out_specs=pl.BlockSpec((ROW_TILE, D_MODEL), lambda i: (i, 0)),
        out_shape=jax.ShapeDtypeStruct((t, D_MODEL), F32),
        scratch_shapes=[pltpu.VMEM((ROW_TILE, D_FF), BF16), pltpu.VMEM((8, 2 * D_FF), F32)],
        compiler_params=_cparams("arbitrary"),
        name="conv_ffn_ln",
    )(x, p["ffn_up"], p["ffn_up_b"], p["ffn_conv"], p["ffn_conv_b"], p["ffn_down"],
      p["ln3_g"], p["ln3_b"])


def _pack_params(w_in, w_out, sg_ln_g, sg_ln_b, sg_w, sg_b, rw_mu, rw_w0, rw_w2, rw_a0, rw_a2, rw_g2,
                 rw_kk, rw_ka, rw_rk, rw_lnx_g, rw_lnx_b, gla_a_up, gla_a_b, gla_norm_g, fox_fb,
                 ln1_g, ln1_b, ca_wq, ca_wk, ca_wv, ca_wo, ln2_g, ln2_b,
                 ffn_up, ffn_up_b, ffn_conv, ffn_conv_b, ffn_down, ln3_g, ln3_b):
    depth = w_in.shape[0]
    row = lambda a: a.reshape(depth, 1, -1)
    gla0, fox0 = 1536, 2320
    misc = jnp.zeros((depth, D_MODEL, 128), F32)
    misc = misc.at[:, :, 0:16].set(w_in[:, :, gla0 + 768:gla0 + 784])
    misc = misc.at[:, :, MISC_FOX_LANE:MISC_FOX_LANE + 4].set(w_in[:, :, fox0 + 768:fox0 + 772])
    w_pack = jnp.concatenate([w_in[:, :, 0:gla0 + 768], w_in[:, :, fox0:fox0 + 768], misc], axis=-1)

    def pad_rows(w, start):
        out = jnp.zeros((depth, GROUP, GROUP), F32)
        return out.at[:, start:start + w.shape[1], :].set(w).astype(BF16)

    fb = jnp.zeros((depth, 1, 128), F32).at[:, 0, MISC_FOX_LANE:MISC_FOX_LANE + 4].set(fox_fb)
    aup = jnp.zeros((depth, 128, N_HEADS * GLA_DK), F32).at[:, 0:16, :].set(gla_a_up)
    return {
        "w_in": w_pack.astype(BF16),
        "w_out": w_out.reshape(depth, 4, GROUP, D_MODEL).astype(BF16),
        "sg_ln_g": row(sg_ln_g), "sg_ln_b": row(sg_ln_b), "sg_w": sg_w,
        "sg_bias": jnp.repeat(jnp.swapaxes(sg_b, 1, 2), HEAD_DIM, axis=2),
        "rw_mu": rw_mu.reshape(depth, 4, 1, GROUP),
        "rw_w0": row(rw_w0), "rw_a0": row(rw_a0),
        "rw_w2": pad_rows(rw_w2, 0), "rw_a2": pad_rows(rw_a2, 64), "rw_g2": pad_rows(rw_g2, 128),
        "rw_kk": row(rw_kk), "rw_ka": row(rw_ka), "rw_rk": row(rw_rk),
        "rw_lnx_g": row(rw_lnx_g), "rw_lnx_b": row(rw_lnx_b),
        "gla_a_up": aup.astype(BF16), "gla_a_b": row(gla_a_b), "gla_norm_g": row(gla_norm_g),
        "fox_fb": fb,
        "ln1_g": row(ln1_g), "ln1_b": row(ln1_b),
        "ca_wq": ca_wq.astype(BF16), "ca_wo": ca_wo.astype(BF16),
        "ca_wkv": jnp.stack([ca_wk, ca_wv], axis=1).reshape(2 * depth, D_MODEL, D_MODEL).astype(BF16),
        "ln2_g": row(ln2_g), "ln2_b": row(ln2_b),
        "ffn_up": ffn_up.astype(BF16), "ffn_up_b": row(ffn_up_b), "ffn_conv": ffn_conv,
        "ffn_conv_b": row(ffn_conv_b), "ffn_down": ffn_down.astype(BF16),
        "ln3_g": row(ln3_g), "ln3_b": row(ln3_b),
    }


def _mixer_layer(x, p, layer, batch, seq):
    z = _inproj(x, p["w_in"], layer)
    y_a = _spatial_gating(z, p["sg_ln_g"], p["sg_ln_b"], p["sg_w"], p["sg_bias"], layer)
    y_b = _rwkv(z, p, layer, batch, seq)
    y_c = _gla(z, p, layer, batch, seq)
    y_d = _fox(*_fox_prep(z, p["fox_fb"], layer, batch, seq), batch, seq)
    return _outproj(y_a, y_b, y_c, y_d, p["w_out"], x, p["ln1_g"], p["ln1_b"], layer)


def kernel(x, mem, mem_ln_g, mem_ln_b, w_in, w_out, sg_ln_g, sg_ln_b, sg_w, sg_b, rw_mu, rw_w0, rw_w2, rw_a0, rw_a2, rw_g2, rw_kk, rw_ka, rw_rk, rw_lnx_g, rw_lnx_b, gla_a_up, gla_a_b, gla_norm_g, fox_fb, ln1_g, ln1_b, ca_wq, ca_wk, ca_wv, ca_wo, ln2_g, ln2_b, ffn_up, ffn_up_b, ffn_conv, ffn_conv_b, ffn_down, ln3_g, ln3_b):
    batch, seq, _ = x.shape
    depth = w_in.shape[0]
    p = _pack_params(w_in, w_out, sg_ln_g, sg_ln_b, sg_w, sg_b, rw_mu, rw_w0, rw_w2, rw_a0, rw_a2,
                     rw_g2, rw_kk, rw_ka, rw_rk, rw_lnx_g, rw_lnx_b, gla_a_up, gla_a_b, gla_norm_g,
                     fox_fb, ln1_g, ln1_b, ca_wq, ca_wk, ca_wv, ca_wo, ln2_g, ln2_b,
                     ffn_up, ffn_up_b, ffn_conv, ffn_conv_b, ffn_down, ln3_g, ln3_b)
    kv = _memkv(mem.reshape(batch * N_MEM, D_MODEL), mem_ln_g.reshape(1, -1), mem_ln_b.reshape(1, -1),
                p["ca_wkv"])
    h = x.reshape(batch * seq, D_MODEL)
    for layer in range(depth):
        h = _mixer_layer(h, p, layer, batch, seq)
        h = _cross_attention(h, kv, p["ca_wq"], p["ca_wo"], p["ln2_g"], p["ln2_b"], layer, batch, seq)
        h = _ffn(h, p, layer, seq)
    return h.reshape(batch, seq, D_MODEL)
```

```python
import functools

import jax
import jax.numpy as jnp
from jax import lax
from jax.experimental import pallas as pl
from jax.experimental.pallas import tpu as pltpu

F32 = jnp.float32
BF16 = jnp.bfloat16

D_MODEL = 1024
DEPTH = 4
N_MEM = 256
HEAD_DIM = 64
N_HEADS = 4
GROUP = 256
SG_CHUNK = 128
RW_LNX_EPS = 64e-5
RW_CHUNK = 64
GLA_DK = 32
GLA_TAU = 16.0
GLA_CHUNK = 64
GLA_NORM_EPS = 1e-5
CA_HEADS = 4
CA_HEAD_DIM = 256
D_FF = 2816
DN_ALPHA = (2.0 * DEPTH) ** 0.25
LN_EPS = 1e-5

P_PACK = 12 * 256 + 128
MISC_BLOCK = 24
MISC_FOX_LANE = 16

ROW_TILE = 512
FFN_COL_TILE = 256
VMEM_LIMIT = 48 * 1024 * 1024


def _cparams(*sem):
    return pltpu.CompilerParams(dimension_semantics=sem, vmem_limit_bytes=VMEM_LIMIT)


def _dot(a, b):
    return jnp.dot(a.astype(BF16), b.astype(BF16), preferred_element_type=F32)


def _dot_nt(a, b):
    return lax.dot_general(a.astype(BF16), b.astype(BF16), (((1,), (1,)), ((), ())),
                           preferred_element_type=F32)


def _dot_tn(a, b):
    return lax.dot_general(a.astype(BF16), b.astype(BF16), (((0,), (0,)), ((), ())),
                           preferred_element_type=F32)


def _split2(a):
    hi = a.astype(BF16)
    lo = (a - hi.astype(F32)).astype(BF16)
    return hi, lo


def _dot3(a, b):
    ah, al = _split2(a)
    bh, bl = _split2(b)
    return (jnp.dot(ah, bh, preferred_element_type=F32)
            + (jnp.dot(ah, bl, preferred_element_type=F32) + jnp.dot(al, bh, preferred_element_type=F32)))


def _group_ones(n, shift):
    r = lax.broadcasted_iota(jnp.int32, (n, n), 0) >> shift
    c = lax.broadcasted_iota(jnp.int32, (n, n), 1) >> shift
    return jnp.where(r == c, 1.0, 0.0).astype(BF16)


def _gsum(x, ones):
    hi, lo = _split2(x)
    return jnp.dot(hi, ones, preferred_element_type=F32) + jnp.dot(lo, ones, preferred_element_type=F32)


def _lane_head(width, shift):
    return lax.broadcasted_iota(jnp.int32, (1, width), 1) >> shift


def _softplus(x):
    return jnp.maximum(x, 0.0) + jnp.log1p(jnp.exp(-jnp.abs(x)))


def _log_sigmoid(x):
    return jnp.minimum(x, 0.0) - jnp.log1p(jnp.exp(-jnp.abs(x)))


def _sigmoid(x):
    return 1.0 / (1.0 + jnp.exp(-x))


def _layernorm(x, g, b):
    mu = jnp.mean(x, axis=-1, keepdims=True)
    d = x - mu
    var = jnp.mean(d * d, axis=-1, keepdims=True)
    return d * lax.rsqrt(var + LN_EPS) * g + b


def _seg_cumsum(x, seg):
    pos = lax.broadcasted_iota(jnp.int32, x.shape, 0) & (seg - 1)
    k = 1
    while k < seg:
        x = x + jnp.where(pos >= k, pltpu.roll(x, k, 0), 0.0)
        k *= 2
    return x


def _stack_heads(x, lane_head):
    return jnp.concatenate([jnp.where(lane_head == h, x, 0.0) for h in range(N_HEADS)], axis=0)


def _unstack_heads(x, c):
    return (x[0:c] + x[c:2 * c]) + (x[2 * c:3 * c] + x[3 * c:4 * c])


def _inproj_kernel(x_ref, w_ref, o_ref):
    o_ref[...] = jnp.dot(x_ref[...].astype(BF16), w_ref[...], preferred_element_type=F32)


def _inproj(x, w_pack, layer):
    t = x.shape[0]
    return pl.pallas_call(
        _inproj_kernel,
        grid=(t // ROW_TILE,),
        in_specs=[pl.BlockSpec((ROW_TILE, D_MODEL), lambda i: (i, 0)),
                  pl.BlockSpec((None, D_MODEL, P_PACK), lambda i: (layer, 0, 0))],
        out_specs=pl.BlockSpec((ROW_TILE, P_PACK), lambda i: (i, 0)),
        out_shape=jax.ShapeDtypeStruct((t, P_PACK), F32),
        compiler_params=_cparams("parallel"),
        name="inproj",
    )(x, w_pack)


def _sg_kernel(u_ref, v_ref, g_ref, b_ref, w_ref, sb_ref, o_ref):
    ones = _group_ones(GROUP, 6)
    lane_head = _lane_head(GROUP, 6)
    v = v_ref[...]
    mean = _gsum(v, ones) * (1.0 / HEAD_DIM)
    d = v - mean
    var = _gsum(d * d, ones) * (1.0 / HEAD_DIM)
    vn = d * lax.rsqrt(var + LN_EPS) * g_ref[...] + b_ref[...]
    row = lax.broadcasted_iota(jnp.int32, (SG_CHUNK, SG_CHUNK), 0)
    col = lax.broadcasted_iota(jnp.int32, (SG_CHUNK, SG_CHUNK), 1)
    wm = [jnp.where(col <= row, w_ref[h], 0.0).astype(BF16) for h in range(N_HEADS)]
    for c in range(ROW_TILE // SG_CHUNK):
        sl = slice(c * SG_CHUNK, (c + 1) * SG_CHUNK)
        vc = vn[sl]
        s = sb_ref[...]
        for h in range(N_HEADS):
            s = s + jnp.dot(wm[h], jnp.where(lane_head == h, vc, 0.0).astype(BF16),
                            preferred_element_type=F32)
        o_ref[sl, :] = u_ref[sl, :] * s


def _spatial_gating(z, ln_g, ln_b, w_s, sbias, layer):
    t = z.shape[0]
    return pl.pallas_call(
        _sg_kernel,
        grid=(t // ROW_TILE,),
        in_specs=[pl.BlockSpec((ROW_TILE, GROUP), lambda i: (i, 0)),
                  pl.BlockSpec((ROW_TILE, GROUP), lambda i: (i, 1)),
                  pl.BlockSpec((None, 1, GROUP), lambda i: (layer, 0, 0)),
                  pl.BlockSpec((None, 1, GROUP), lambda i: (layer, 0, 0)),
                  pl.BlockSpec((None, N_HEADS, SG_CHUNK, SG_CHUNK), lambda i: (layer, 0, 0, 0)),
                  pl.BlockSpec((None, SG_CHUNK, GROUP), lambda i: (layer, 0, 0))],
        out_specs=pl.BlockSpec((ROW_TILE, GROUP), lambda i: (i, 0)),
        out_shape=jax.ShapeDtypeStruct((t, GROUP), F32),
        compiler_params=_cparams("parallel"),
        name="spatial_gating",
    )(z, z, ln_g, ln_b, w_s, sbias)


def _tri_inverse(mats, same_head, wt, ws):
    expand = lambda w: jnp.where(same_head, jnp.concatenate([w] * N_HEADS, axis=0), 0.0).astype(BF16)
    b16 = (wt >> 4) == (ws >> 4)
    b32 = (wt >> 5) == (ws >> 5)
    in32 = b32 & jnp.logical_not(b16)
    in64 = jnp.logical_not(b32)
    x = [jnp.where(b16, a, 0.0) for a in mats]
    t = [jnp.where(wt == ws, 1.0, 0.0) + d for d in x]
    for _ in range(3):
        x = [_dot(d, expand(d)) for d in x]
        t = [ti + _dot(ti, expand(d)) for ti, d in zip(t, x)]
    for level in (in32, in64):
        te = [_dot(ti, expand(jnp.where(level, a, 0.0))) for ti, a in zip(t, mats)]
        t = [ti + _dot(tei, expand(ti)) for ti, tei in zip(t, te)]
    return t


def _rwkv_kernel(r_ref, k_ref, v_ref, lo_ref, mu_ref, w0_ref, a0_ref, w2_ref, a2_ref, g2_ref,
                 kk_ref, ka_ref, rk_ref, lng_ref, lnb_ref, o_ref,
                 prev_ref, st_ref, at_s, bt_s, kt_s, rt_s, v_s, wc_s, y_s, rf_s, g_s, n_s, h0_s):
    c_len = RW_CHUNK
    n_chunks = ROW_TILE // c_len

    @pl.when(pl.program_id(1) == 0)
    def _():
        prev_ref[...] = jnp.zeros_like(prev_ref)
        st_ref[...] = jnp.zeros_like(st_ref)

    ones = _group_ones(GROUP, 6)
    lane_head = _lane_head(GROUP, 6)
    row0 = lax.broadcasted_iota(jnp.int32, (ROW_TILE, GROUP), 0) == 0

    def shifted(ref, j):
        z = ref[...]
        prev = prev_ref[j]
        zs = jnp.where(row0, prev[7:8, :], pltpu.roll(z, 1, 0))
        prev_ref[j] = z[ROW_TILE - 8:ROW_TILE, :]
        return z + mu_ref[j] * (zs - z)

    r = shifted(r_ref, 0)
    k = shifted(k_ref, 1)
    v = shifted(v_ref, 2)
    lo = shifted(lo_ref, 3)

    wlog = -_softplus(-(w0_ref[...] + _dot(jnp.tanh(lo), w2_ref[...]))) - 0.5
    ld = -jnp.exp(wlog)
    a = _sigmoid(a0_ref[...] + _dot(lo, a2_ref[...]))
    gate = _dot(_sigmoid(lo), g2_ref[...])
    kk = k * kk_ref[...]
    kk = kk / jnp.maximum(jnp.sqrt(_gsum(kk * kk, ones)), 1e-12)
    k = k * (1.0 + (a - 1.0) * ka_ref[...])
    bonus = _gsum(r * k * rk_ref[...], ones) * v

    cl = _seg_cumsum(ld, c_len)
    e_pos = jnp.exp(cl)
    e_neg = jnp.exp(-cl)
    at_s[...] = -kk * jnp.exp(cl - ld)
    bt_s[...] = kk * a * e_neg
    kt_s[...] = k * e_neg
    rt_s[...] = r * e_pos
    v_s[...] = v
    wc_s[...] = e_pos

    n4 = N_HEADS * c_len
    ri = lax.broadcasted_iota(jnp.int32, (n4, n4), 0)
    ci = lax.broadcasted_iota(jnp.int32, (n4, n4), 1)
    same_head = (ri >> 6) == (ci >> 6)
    wt = lax.broadcasted_iota(jnp.int32, (c_len, n4), 0)
    ws = lax.broadcasted_iota(jnp.int32, (c_len, n4), 1) & (c_len - 1)
    strict_w = ws < wt
    incl_w = ws <= wt

    chunks = range(n_chunks)
    rows = [slice(c * c_len, (c + 1) * c_len) for c in chunks]
    gram = [_dot_nt(jnp.concatenate([at_s[sl, :], rt_s[sl, :]], axis=0),
                    jnp.concatenate([_stack_heads(bt_s[sl, :], lane_head),
                                     _stack_heads(kt_s[sl, :], lane_head)], axis=0)) for sl in rows]
    a_ab = [jnp.where(strict_w, g[0:c_len, 0:n4], 0.0) for g in gram]
    t_wide = _tri_inverse(a_ab, same_head, wt, ws)
    for c, sl in enumerate(rows):
        g = gram[c]
        a_ak = jnp.where(strict_w, g[0:c_len, n4:2 * n4], 0.0)
        l_rb = jnp.where(incl_w, g[c_len:2 * c_len, 0:n4], 0.0)
        l_rk = jnp.where(incl_w, g[c_len:2 * c_len, n4:2 * n4], 0.0)
        t_w = t_wide[c]
        v_stack = _stack_heads(v_s[sl, :], lane_head)
        ta = _dot(t_w, _stack_heads(at_s[sl, :], lane_head))
        uv = _dot(t_w, _stack_heads(_dot(a_ak, v_stack), lane_head))
        rf_s[sl, :] = rt_s[sl, :] + _dot(l_rb, _stack_heads(ta, lane_head))
        y_s[sl, :] = _dot(l_rb, _stack_heads(uv, lane_head)) + _dot(l_rk, v_stack)
        wc = wc_s[(c + 1) * c_len - 1:(c + 1) * c_len, :]
        bt = bt_s[sl, :]
        g_s[c] = (jnp.where(same_head, _dot_tn(ta, bt), 0.0) * wc).astype(BF16)
        n_s[c] = jnp.where(same_head, _dot_tn(jnp.concatenate([uv, v_s[sl, :]], axis=0),
                                              jnp.concatenate([bt, kt_s[sl, :]], axis=0)), 0.0) * wc

    for c in chunks:
        st = st_ref[...]
        stb = st.astype(BF16)
        h0_s[c] = stb
        wc = wc_s[(c + 1) * c_len - 1:(c + 1) * c_len, :]
        st_ref[...] = st * wc + jnp.dot(stb, g_s[c], preferred_element_type=F32) + n_s[c]

    for c, sl in enumerate(rows):
        y_s[sl, :] = y_s[sl, :] + _dot_nt(rf_s[sl, :], h0_s[c])

    y = y_s[...]
    mean = _gsum(y, ones) * (1.0 / HEAD_DIM)
    d = y - mean
    var = _gsum(d * d, ones) * (1.0 / HEAD_DIM)
    y = d * lax.rsqrt(var + RW_LNX_EPS) * lng_ref[...] + lnb_ref[...]
    o_ref[...] = (y + bonus) * gate


def _rwkv(z, p, layer, batch, seq):
    t = z.shape[0]
    tiles = seq // ROW_TILE
    row = lambda c: pl.BlockSpec((ROW_TILE, GROUP), lambda b, i, c=c: (b * tiles + i, c))
    vec = pl.BlockSpec((None, 1, GROUP), lambda b, i: (layer, 0, 0))
    mat = pl.BlockSpec((None, GROUP, GROUP), lambda b, i: (layer, 0, 0))
    scr = pltpu.VMEM((ROW_TILE, GROUP), F32)
    n_chunks = ROW_TILE // RW_CHUNK
    return pl.pallas_call(
        _rwkv_kernel,
        grid=(batch, tiles),
        in_specs=[row(2), row(3), row(4), row(5),
                  pl.BlockSpec((None, 4, 1, GROUP), lambda b, i: (layer, 0, 0, 0)),
                  vec, vec, mat, mat, mat, vec, vec, vec, vec, vec],
        out_specs=pl.BlockSpec((ROW_TILE, GROUP), lambda b, i: (b * tiles + i, 0)),
        out_shape=jax.ShapeDtypeStruct((t, GROUP), F32),
        scratch_shapes=[pltpu.VMEM((4, 8, GROUP), F32), pltpu.VMEM((GROUP, GROUP), F32),
                        scr, scr, scr, scr, scr, scr, scr, scr,
                        pltpu.VMEM((n_chunks, GROUP, GROUP), BF16),
                        pltpu.VMEM((n_chunks, GROUP, GROUP), F32),
                        pltpu.VMEM((n_chunks, GROUP, GROUP), BF16)],
        compiler_params=_cparams("parallel", "arbitrary"),
        name="rwkv7",
    )(z, z, z, z, p["rw_mu"], p["rw_w0"], p["rw_a0"], p["rw_w2"], p["rw_a2"], p["rw_g2"],
      p["rw_kk"], p["rw_ka"], p["rw_rk"], p["rw_lnx_g"], p["rw_lnx_b"])


def _gla_kernel(qk_ref, v_ref, g_ref, m_ref, aup_ref, ab_ref, ng_ref, o_ref,
                st_ref, qe_s, ke_s, k2_s, qi_s, dec_s, o_s):
    c_len = GLA_CHUNK
    n_chunks = ROW_TILE // c_len
    kw = N_HEADS * GLA_DK

    @pl.when(pl.program_id(1) == 0)
    def _():
        st_ref[...] = jnp.zeros_like(st_ref)

    lg = _log_sigmoid(_dot(m_ref[...], aup_ref[...]) + ab_ref[...]) * (1.0 / GLA_TAU)
    b = _seg_cumsum(lg, c_len).reshape(n_chunks, c_len, kw)
    b_mid = b[:, c_len // 2:c_len // 2 + 1, :]
    b_last = b[:, c_len - 1:c_len, :]
    q = (qk_ref[:, 0:kw] * (GLA_DK ** -0.5)).reshape(n_chunks, c_len, kw)
    k = qk_ref[:, kw:2 * kw].reshape(n_chunks, c_len, kw)
    qe_s[...] = (q * jnp.exp(b - b_mid)).reshape(ROW_TILE, kw)
    ke_s[...] = (k * jnp.exp(b_mid - b)).reshape(ROW_TILE, kw)
    k2_s[...] = (k * jnp.exp(b_last - b)).reshape(ROW_TILE, kw)
    qi_s[...] = (q * jnp.exp(b)).reshape(ROW_TILE, kw)
    dec_s[...] = jnp.exp(b).reshape(ROW_TILE, kw)

    k_head = _lane_head(kw, 5)
    v_head = _lane_head(GROUP, 6)
    causal = ((lax.broadcasted_iota(jnp.int32, (c_len, N_HEADS * c_len), 1) & (c_len - 1))
              <= lax.broadcasted_iota(jnp.int32, (c_len, N_HEADS * c_len), 0))
    same_head = ((lax.broadcasted_iota(jnp.int32, (GROUP, kw), 0) >> 6)
                 == (lax.broadcasted_iota(jnp.int32, (GROUP, kw), 1) >> 5))

    rows = [slice(c * c_len, (c + 1) * c_len) for c in range(n_chunks)]
    inc = [jnp.where(same_head, _dot_tn(v_ref[sl, :], k2_s[sl, :]), 0.0) for sl in rows]
    st = st_ref[...]
    entering = []
    for c in range(n_chunks):
        entering.append(st.astype(BF16))
        st = st * dec_s[(c + 1) * c_len - 1:(c + 1) * c_len, :] + inc[c]
    st_ref[...] = st
    for c, sl in enumerate(rows):
        att = jnp.where(causal, _dot_nt(qe_s[sl, :], _stack_heads(ke_s[sl, :], k_head)), 0.0)
        o_s[sl, :] = _dot(att, _stack_heads(v_ref[sl, :], v_head)) + _dot_nt(qi_s[sl, :], entering[c])

    o = o_s[...]
    ms = _gsum(o * o, _group_ones(GROUP, 6)) * (1.0 / HEAD_DIM)
    g = g_ref[...]
    o_ref[...] = o * lax.rsqrt(ms + GLA_NORM_EPS) * ng_ref[...] * (g * _sigmoid(g))


def _gla(z, p, layer, batch, seq):
    t = z.shape[0]
    tiles = seq // ROW_TILE
    kw = N_HEADS * GLA_DK
    row = lambda c: pl.BlockSpec((ROW_TILE, GROUP), lambda b, i, c=c: (b * tiles + i, c))
    kscr = pltpu.VMEM((ROW_TILE, kw), F32)
    return pl.pallas_call(
        _gla_kernel,
        grid=(batch, tiles),
        in_specs=[row(6), row(7), row(8),
                  pl.BlockSpec((ROW_TILE, 128), lambda b, i: (b * tiles + i, MISC_BLOCK)),
                  pl.BlockSpec((None, 128, kw), lambda b, i: (layer, 0, 0)),
                  pl.BlockSpec((None, 1, kw), lambda b, i: (layer, 0, 0)),
                  pl.BlockSpec((None, 1, GROUP), lambda b, i: (layer, 0, 0))],
        out_specs=pl.BlockSpec((ROW_TILE, GROUP), lambda b, i: (b * tiles + i, 0)),
        out_shape=jax.ShapeDtypeStruct((t, GROUP), F32),
        scratch_shapes=[pltpu.VMEM((GROUP, kw), F32), kscr, kscr, kscr, kscr, kscr,
                        pltpu.VMEM((ROW_TILE, GROUP), F32)],
        compiler_params=_cparams("parallel", "arbitrary"),
        name="gla",
    )(z, z, z, z, p["gla_a_up"], p["gla_a_b"], p["gla_norm_g"])


def _fox_prep_kernel(q_ref, k_ref, v_ref, z_ref, fb_ref, qa_ref, ka_ref, vb_ref):
    c = _seg_cumsum(_log_sigmoid(z_ref[...] + fb_ref[...]), z_ref.shape[0])
    lane = lax.broadcasted_iota(jnp.int32, (1, 128), 1)
    for h in range(N_HEADS):
        ch = c[:, MISC_FOX_LANE + h:MISC_FOX_LANE + h + 1]
        c1 = ch.astype(BF16).astype(F32)
        c2 = (ch - c1).astype(BF16).astype(F32)
        c3 = ch - c1 - c2
        tile = slice((h // 2) * 128, (h // 2 + 1) * 128)
        off = (h % 2) * HEAD_DIM
        aug = HEAD_DIM - off
        in_head = (lane >> 6) == (h % 2)
        qa = jnp.where((lane >= aug + 3) & (lane < aug + 6), 1.0, 0.0)
        ka = jnp.where((lane >= aug) & (lane < aug + 3), 1.0, 0.0)
        for n, cn in enumerate((c1, c2, c3)):
            qa = jnp.where(lane == aug + n, cn, qa)
            ka = jnp.where(lane == aug + 3 + n, -cn, ka)
        qa = jnp.where(in_head, q_ref[:, tile] * (HEAD_DIM ** -0.5), qa)
        ka = jnp.where(in_head, k_ref[:, tile], ka)
        qa_ref[h] = qa.astype(BF16)
        ka_ref[h] = ka.astype(BF16)
    vb_ref[...] = v_ref[...].astype(BF16)


def _fox_prep(z, fb, layer, batch, seq):
    t = z.shape[0]
    col = lambda c: pl.BlockSpec((seq, GROUP), lambda b, c=c: (b, c))
    head = pl.BlockSpec((N_HEADS, seq, 128), lambda b: (0, b, 0))
    return pl.pallas_call(
        _fox_prep_kernel,
        grid=(batch,),
        in_specs=[col(9), col(10), col(11),
                  pl.BlockSpec((seq, 128), lambda b: (b, MISC_BLOCK)),
                  pl.BlockSpec((None, 1, 128), lambda b: (layer, 0, 0))],
        out_specs=[head, head, pl.BlockSpec((seq, GROUP), lambda b: (b, 0))],
        out_shape=[jax.ShapeDtypeStruct((N_HEADS, t, 128), BF16),
                   jax.ShapeDtypeStruct((N_HEADS, t, 128), BF16),
                   jax.ShapeDtypeStruct((t, GROUP), BF16)],
        compiler_params=_cparams("parallel"),
        name="fox_prep",
    )(z, z, z, z, fb)


def _fox_kernel(qi_ref, kj_ref, qa_ref, ka_ref, v_ref, o_ref, m_ref, l_ref, acc_ref):
    tq = ROW_TILE
    i = qi_ref[pl.program_id(1)]
    j = kj_ref[pl.program_id(1)]

    @pl.when(j == 0)
    def _():
        m_ref[...] = jnp.full_like(m_ref, -1e30)
        l_ref[...] = jnp.zeros_like(l_ref)
        acc_ref[...] = jnp.zeros_like(acc_ref)

    lane_head = _lane_head(GROUP, 6)

    def update(on_diagonal):
        vb = v_ref[...]
        acc = acc_ref[...]
        if on_diagonal:
            causal = (lax.broadcasted_iota(jnp.int32, (tq, tq), 1)
                      <= lax.broadcasted_iota(jnp.int32, (tq, tq), 0))
        heads = range(N_HEADS)
        s = [lax.dot_general(qa_ref[h], ka_ref[h], (((1,), (1,)), ((), ())),
                             preferred_element_type=F32) for h in heads]
        if on_diagonal:
            s = [jnp.where(causal, s[h], -1e30) for h in heads]
        m_prev = [m_ref[h] for h in heads]
        m_new = [jnp.maximum(m_prev[h], jnp.max(s[h], axis=-1, keepdims=True)) for h in heads]
        alpha = [jnp.exp(m_prev[h] - m_new[h]) for h in heads]
        p = [jnp.exp(s[h] - jnp.concatenate([m_new[h]] * (tq // 128), axis=1)) for h in heads]
        for h in heads:
            l_ref[h] = alpha[h] * l_ref[h] + jnp.sum(p[h], axis=-1, keepdims=True)
            m_ref[h] = m_new[h]
        pv = [jnp.dot(p[h].astype(BF16), vb, preferred_element_type=F32) for h in heads]
        pv = jnp.where(lane_head < 2, jnp.where(lane_head == 0, pv[0], pv[1]),
                       jnp.where(lane_head == 2, pv[2], pv[3]))
        return acc * per_head(alpha) + pv

    first_half = lax.broadcasted_iota(jnp.int32, (1, 128), 1) < HEAD_DIM

    def per_head(xs):
        return jnp.concatenate([jnp.where(first_half, xs[0], xs[1]),
                                jnp.where(first_half, xs[2], xs[3])], axis=1)

    @pl.when(j < i)
    def _():
        acc_ref[...] = update(False)

    @pl.when(j == i)
    def _():
        acc = update(True)
        o_ref[...] = acc * per_head([1.0 / l_ref[h] for h in range(N_HEADS)])


def _fox(qa, ka, vb, batch, seq):
    t = vb.shape[0]
    tiles = seq // ROW_TILE
    pairs = [(i, j) for i in range(tiles) for j in range(i + 1)]
    q_tile = jnp.asarray([i for i, _ in pairs], jnp.int32)
    k_tile = jnp.asarray([j for _, j in pairs], jnp.int32)
    return pl.pallas_call(
        _fox_kernel,
        grid_spec=pltpu.PrefetchScalarGridSpec(
            num_scalar_prefetch=2,
            grid=(batch, len(pairs)),
            in_specs=[pl.BlockSpec((N_HEADS, ROW_TILE, 128),
                                   lambda b, t, qi, kj: (0, b * tiles + qi[t], 0)),
                      pl.BlockSpec((N_HEADS, ROW_TILE, 128),
                                   lambda b, t, qi, kj: (0, b * tiles + kj[t], 0)),
                      pl.BlockSpec((ROW_TILE, GROUP), lambda b, t, qi, kj: (b * tiles + kj[t], 0))],
            out_specs=pl.BlockSpec((ROW_TILE, GROUP), lambda b, t, qi, kj: (b * tiles + qi[t], 0)),
            scratch_shapes=[pltpu.VMEM((N_HEADS, ROW_TILE, 128), F32),
                            pltpu.VMEM((N_HEADS, ROW_TILE, 128), F32),
                            pltpu.VMEM((ROW_TILE, GROUP), F32)]),
        out_shape=jax.ShapeDtypeStruct((t, GROUP), F32),
        compiler_params=_cparams("parallel", "arbitrary"),
        name="fox_attention",
    )(q_tile, k_tile, qa, ka, vb)


def _outproj_kernel(ya_ref, yb_ref, yc_ref, yd_ref, w_ref, x_ref, g_ref, b_ref, o_ref):
    mix = _dot(ya_ref[...], w_ref[0]) + _dot(yb_ref[...], w_ref[1])
    mix = mix + (_dot(yc_ref[...], w_ref[2]) + _dot(yd_ref[...], w_ref[3]))
    o_ref[...] = _layernorm(DN_ALPHA * x_ref[...] + mix, g_ref[...], b_ref[...])


def _outproj(ya, yb, yc, yd, w_out, x, g, b, layer):
    t = x.shape[0]
    y_spec = pl.BlockSpec((ROW_TILE, GROUP), lambda i: (i, 0))
    vec = pl.BlockSpec((None, 1, D_MODEL), lambda i: (layer, 0, 0))
    return pl.pallas_call(
        _outproj_kernel,
        grid=(t // ROW_TILE,),
        in_specs=[y_spec, y_spec, y_spec, y_spec,
                  pl.BlockSpec((None, 4, GROUP, D_MODEL), lambda i: (layer, 0, 0, 0)),
                  pl.BlockSpec((ROW_TILE, D_MODEL), lambda i: (i, 0)), vec, vec],
        out_specs=pl.BlockSpec((ROW_TILE, D_MODEL), lambda i: (i, 0)),
        out_shape=jax.ShapeDtypeStruct((t, D_MODEL), F32),
        compiler_params=_cparams("parallel"),
        name="outproj_ln",
    )(ya, yb, yc, yd, w_out, x, g, b)


def _memkv_kernel(mem_ref, g_ref, b_ref, w_ref, o_ref):
    memn = _layernorm(mem_ref[...], g_ref[...], b_ref[...])
    o_ref[...] = _dot(memn, w_ref[...]).astype(BF16)


def _memkv(mem, g, b, w_kv):
    rows = mem.shape[0]
    n = w_kv.shape[0]
    return pl.pallas_call(
        _memkv_kernel,
        grid=(n,),
        in_specs=[pl.BlockSpec((rows, D_MODEL), lambda i: (0, 0)),
                  pl.BlockSpec((1, D_MODEL), lambda i: (0, 0)),
                  pl.BlockSpec((1, D_MODEL), lambda i: (0, 0)),
                  pl.BlockSpec((None, D_MODEL, D_MODEL), lambda i: (i, 0, 0))],
        out_specs=pl.BlockSpec((None, rows, D_MODEL), lambda i: (i, 0, 0)),
        out_shape=jax.ShapeDtypeStruct((n, rows, D_MODEL), BF16),
        compiler_params=_cparams("parallel"),
        name="memory_kv",
    )(mem, g, b, w_kv)


def _cross_kernel(x_ref, k_ref, v_ref, wq_ref, wo_ref, g_ref, b_ref, o_ref):
    x = x_ref[...]
    q = jnp.dot(x.astype(BF16), wq_ref[...], preferred_element_type=F32).astype(BF16)
    heads = range(CA_HEADS)
    cols = [slice(h * CA_HEAD_DIM, (h + 1) * CA_HEAD_DIM) for h in heads]
    s = [_dot_nt(q[:, cols[h]], k_ref[:, cols[h]]) * (CA_HEAD_DIM ** -0.5) for h in heads]
    p = [jnp.exp(s[h] - jnp.max(s[h], axis=-1, keepdims=True)) for h in heads]
    p = [p[h] * (1.0 / jnp.sum(p[h], axis=-1, keepdims=True)) for h in heads]
    o = [jnp.dot(p[h].astype(BF16), v_ref[:, cols[h]], preferred_element_type=F32).astype(BF16)
         for h in heads]
    ca = jnp.dot(jnp.concatenate(o, axis=1), wo_ref[...], preferred_element_type=F32)
    o_ref[...] = _layernorm(DN_ALPHA * x + ca, g_ref[...], b_ref[...])


def _cross_attention(x, kv, wq, wo, g, b, layer, batch, seq):
    t = x.shape[0]
    tiles = seq // ROW_TILE
    vec = pl.BlockSpec((None, 1, D_MODEL), lambda bb, i: (layer, 0, 0))
    mat = pl.BlockSpec((None, D_MODEL, D_MODEL), lambda bb, i: (layer, 0, 0))
    return pl.pallas_call(
        _cross_kernel,
        grid=(batch, tiles),
        in_specs=[pl.BlockSpec((ROW_TILE, D_MODEL), lambda bb, i: (bb * tiles + i, 0)),
                  pl.BlockSpec((None, N_MEM, D_MODEL), lambda bb, i: (2 * layer, bb, 0)),
                  pl.BlockSpec((None, N_MEM, D_MODEL), lambda bb, i: (2 * layer + 1, bb, 0)),
                  mat, mat, vec, vec],
        out_specs=pl.BlockSpec((ROW_TILE, D_MODEL), lambda bb, i: (bb * tiles + i, 0)),
        out_shape=jax.ShapeDtypeStruct((t, D_MODEL), F32),
        compiler_params=_cparams("parallel", "parallel"),
        name="cross_attention",
    )(x, kv, kv, wq, wo, g, b)


def _ffn_kernel(x_ref, wu_ref, bu_ref, cw_ref, cb_ref, wd_ref, g_ref, b_ref, o_ref,
                act_ref, carry_ref, *, tiles_per_seq):
    tm = ROW_TILE
    tf = FFN_COL_TILE
    first = (pl.program_id(0) % tiles_per_seq) == 0

    @pl.when(first)
    def _():
        carry_ref[...] = jnp.zeros_like(carry_ref)

    x = x_ref[...]
    xb = x.astype(BF16)
    row8 = lax.broadcasted_iota(jnp.int32, (8, tf), 0)

    def conv(col):
        cols = slice(col, col + tf)
        h = jnp.dot(xb, wu_ref[:, cols], preferred_element_type=F32) + bu_ref[:, cols]
        prev = carry_ref[:, cols]
        carry_ref[:, cols] = h[tm - 8:tm, :]
        h1 = pltpu.roll(h, 1, 0)
        h2 = pltpu.roll(h, 2, 0)
        top1 = jnp.where(row8 == 0, prev[7:8, :], h1[0:8, :])
        top2 = jnp.where(row8 == 0, prev[6:7, :], jnp.where(row8 == 1, prev[7:8, :], h2[0:8, :]))
        h1 = jnp.concatenate([top1, h1[8:tm, :]], axis=0)
        h2 = jnp.concatenate([top2, h2[8:tm, :]], axis=0)
        cw = cw_ref[:, cols]
        return cw[0:1, :] * h2 + cw[1:2, :] * h1 + cw[2:3, :] * h + cb_ref[:, cols]

    for c in range(D_FF // tf):
        gate = conv(c * tf)
        val = conv(D_FF + c * tf)
        act = 0.5 * gate * (1.0 + lax.erf(gate * (2.0 ** -0.5))) * val
        act_ref[:, c * tf:(c + 1) * tf] = act.astype(BF16)

    ff = jnp.dot(act_ref[...], wd_ref[...], preferred_element_type=F32)
    o_ref[...] = _layernorm(DN_ALPHA * x + ff, g_ref[...], b_ref[...])


def _ffn(x, p, layer, seq):
    t = x.shape[0]
    once = pl.Buffered(1)
    vec = pl.BlockSpec((None, 1, D_MODEL), lambda i: (layer, 0, 0))
    wide = lambda rows: pl.BlockSpec((None, rows, 2 * D_FF), lambda i: (layer, 0, 0))
    return pl.pallas_call(
        functools.partial(_ffn_kernel, tiles_per_seq=seq // ROW_TILE),
        grid=(t // ROW_TILE,),
        in_specs=[pl.BlockSpec((ROW_TILE, D_MODEL), lambda i: (i, 0)),
                  pl.BlockSpec((None, D_MODEL, 2 * D_FF), lambda i: (layer, 0, 0), pipeline_mode=once),
                  wide(1), wide(3), wide(1),
                  pl.BlockSpec((None, D_FF, D_MODEL), lambda i: (layer, 0, 0), pipeline_mode=once),
                  vec, vec],
        out_specs=pl.BlockSpec((ROW_TILE, D_MODEL), lambda i: (i, 0)),
        out_shape=jax.ShapeDtypeStruct((t, D_MODEL), F32),
        scratch_shapes=[pltpu.VMEM((ROW_TILE, D_FF), BF16), pltpu.VMEM((8, 2 * D_FF), F32)],
        compiler_params=_cparams("arbitrary"),
        name="conv_ffn_ln",
    )(x, p["ffn_up"], p["ffn_up_b"], p["ffn_conv"], p["ffn_conv_b"], p["ffn_down"],
      p["ln3_g"], p["ln3_b"])


def _pack_params(w_in, w_out, sg_ln_g, sg_ln_b, sg_w, sg_b, rw_mu, rw_w0, rw_w2, rw_a0, rw_a2, rw_g2,
                 rw_kk, rw_ka, rw_rk, rw_lnx_g, rw_lnx_b, gla_a_up, gla_a_b, gla_norm_g, fox_fb,
                 ln1_g, ln1_b, ca_wq, ca_wk, ca_wv, ca_wo, ln2_g, ln2_b,
                 ffn_up, ffn_up_b, ffn_conv, ffn_conv_b, ffn_down, ln3_g, ln3_b):
    depth = w_in.shape[0]
    row = lambda a: a.reshape(depth, 1, -1)
    gla0, fox0 = 1536, 2320
    misc = jnp.zeros((depth, D_MODEL, 128), F32)
    misc = misc.at[:, :, 0:16].set(w_in[:, :, gla0 + 768:gla0 + 784])
    misc = misc.at[:, :, MISC_FOX_LANE:MISC_FOX_LANE + 4].set(w_in[:, :, fox0 + 768:fox0 + 772])
    w_pack = jnp.concatenate([w_in[:, :, 0:gla0 + 768], w_in[:, :, fox0:fox0 + 768], misc], axis=-1)

    def pad_rows(w, start):
        out = jnp.zeros((depth, GROUP, GROUP), F32)
        return out.at[:, start:start + w.shape[1], :].set(w).astype(BF16)

    fb = jnp.zeros((depth, 1, 128), F32).at[:, 0, MISC_FOX_LANE:MISC_FOX_LANE + 4].set(fox_fb)
    aup = jnp.zeros((depth, 128, N_HEADS * GLA_DK), F32).at[:, 0:16, :].set(gla_a_up)
    return {
        "w_in": w_pack.astype(BF16),
        "w_out": w_out.reshape(depth, 4, GROUP, D_MODEL).astype(BF16),
        "sg_ln_g": row(sg_ln_g), "sg_ln_b": row(sg_ln_b), "sg_w": sg_w,
        "sg_bias": jnp.repeat(jnp.swapaxes(sg_b, 1, 2), HEAD_DIM, axis=2),
        "rw_mu": rw_mu.reshape(depth, 4, 1, GROUP),
        "rw_w0": row(rw_w0), "rw_a0": row(rw_a0),
        "rw_w2": pad_rows(rw_w2, 0), "rw_a2": pad_rows(rw_a2, 64), "rw_g2": pad_rows(rw_g2, 128),
        "rw_kk": row(rw_kk), "rw_ka": row(rw_ka), "rw_rk": row(rw_rk),
        "rw_lnx_g": row(rw_lnx_g), "rw_lnx_b": row(rw_lnx_b),
        "gla_a_up": aup.astype(BF16), "gla_a_b": row(gla_a_b), "gla_norm_g": row(gla_norm_g),
        "fox_fb": fb,
        "ln1_g": row(ln1_g), "ln1_b": row(ln1_b),
        "ca_wq": ca_wq.astype(BF16), "ca_wo": ca_wo.astype(BF16),
        "ca_wkv": jnp.stack([ca_wk, ca_wv], axis=1).reshape(2 * depth, D_MODEL, D_MODEL).astype(BF16),
        "ln2_g": row(ln2_g), "ln2_b": row(ln2_b),
        "ffn_up": ffn_up.astype(BF16), "ffn_up_b": row(ffn_up_b), "ffn_conv": ffn_conv,
        "ffn_conv_b": row(ffn_conv_b), "ffn_down": ffn_down.astype(BF16),
        "ln3_g": row(ln3_g), "ln3_b": row(ln3_b),
    }


def _mixer_layer(x, p, layer, batch, seq):
    z = _inproj(x, p["w_in"], layer)
    y_a = _spatial_gating(z, p["sg_ln_g"], p["sg_ln_b"], p["sg_w"], p["sg_bias"], layer)
    y_b = _rwkv(z, p, layer, batch, seq)
    y_c = _gla(z, p, layer, batch, seq)
    y_d = _fox(*_fox_prep(z, p["fox_fb"], layer, batch, seq), batch, seq)
    return _outproj(y_a, y_b, y_c, y_d, p["w_out"], x, p["ln1_g"], p["ln1_b"], layer)


def kernel(x, mem, mem_ln_g, mem_ln_b, w_in, w_out, sg_ln_g, sg_ln_b, sg_w, sg_b, rw_mu, rw_w0, rw_w2, rw_a0, rw_a2, rw_g2, rw_kk, rw_ka, rw_rk, rw_lnx_g, rw_lnx_b, gla_a_up, gla_a_b, gla_norm_g, fox_fb, ln1_g, ln1_b, ca_wq, ca_wk, ca_wv, ca_wo, ln2_g, ln2_b, ffn_up, ffn_up_b, ffn_conv, ffn_conv_b, ffn_down, ln3_g, ln3_b):
    batch, seq, _ = x.shape
    depth = w_in.shape[0]
    p = _pack_params(w_in, w_out, sg_ln_g, sg_ln_b, sg_w, sg_b, rw_mu, rw_w0, rw_w2, rw_a0, rw_a2,
                     rw_g2, rw_kk, rw_ka, rw_rk, rw_lnx_g, rw_lnx_b, gla_a_up, gla_a_b, gla_norm_g,
                     fox_fb, ln1_g, ln1_b, ca_wq, ca_wk, ca_wv, ca_wo, ln2_g, ln2_b,
                     ffn_up, ffn_up_b, ffn_conv, ffn_conv_b, ffn_down, ln3_g, ln3_b)
    kv = _memkv(mem.reshape(batch * N_MEM, D_MODEL), mem_ln_g.reshape(1, -1), mem_ln_b.reshape(1, -1),
                p["ca_wkv"])
    h = x.reshape(batch * seq, D_MODEL)
    for layer in range(depth):
        h = _mixer_layer(h, p, layer, batch, seq)
        h = _cross_attention(h, kv, p["ca_wq"], p["ca_wo"], p["ln2_g"], p["ln2_b"], layer, batch, seq)
        h = _ffn(h, p, layer, seq)
    return h.reshape(batch, seq, D_MODEL)
```

```python
import functools
import itertools

import jax
import jax.numpy as jnp
from jax import lax
from jax.experimental import pallas as pl
from jax.experimental.pallas import tpu as pltpu

F32 = jnp.float32
BF16 = jnp.bfloat16

D_MODEL = 1024
DEPTH = 4
N_MEM = 256
HEAD_DIM = 64
N_HEADS = 4
GROUP = 256
SG_CHUNK = 128
RW_LNX_EPS = 64e-5
RW_CHUNK = 64
GLA_DK = 32
GLA_TAU = 16.0
GLA_CHUNK = 64
GLA_NORM_EPS = 1e-5
CA_HEADS = 4
CA_HEAD_DIM = 256
D_FF = 2816
DN_ALPHA = (2.0 * DEPTH) ** 0.25
LN_EPS = 1e-5

P_PACK = 12 * 256 + 128
MISC_BLOCK = 24
MISC_FOX_LANE = 16

ROW_TILE = 512
MIX_SEQS = 2
MIX_ROWS = MIX_SEQS * ROW_TILE
FFN_COL_TILE = 256
VMEM_LIMIT = 56 * 1024 * 1024


def _cparams(*sem):
    return pltpu.CompilerParams(dimension_semantics=sem, vmem_limit_bytes=VMEM_LIMIT)


def _dot(a, b):
    return jnp.dot(a.astype(BF16), b.astype(BF16), preferred_element_type=F32)


def _dot_nt(a, b):
    return lax.dot_general(a.astype(BF16), b.astype(BF16), (((1,), (1,)), ((), ())),
                           preferred_element_type=F32)


def _dot_tn(a, b):
    return lax.dot_general(a.astype(BF16), b.astype(BF16), (((0,), (0,)), ((), ())),
                           preferred_element_type=F32)


def _split2(a):
    hi = a.astype(BF16)
    lo = (a - hi.astype(F32)).astype(BF16)
    return hi, lo


def _dot3(a, b):
    ah, al = _split2(a)
    bh, bl = _split2(b)
    return (jnp.dot(ah, bh, preferred_element_type=F32)
            + (jnp.dot(ah, bl, preferred_element_type=F32) + jnp.dot(al, bh, preferred_element_type=F32)))


def _group_ones(n, shift):
    r = lax.broadcasted_iota(jnp.int32, (n, n), 0) >> shift
    c = lax.broadcasted_iota(jnp.int32, (n, n), 1) >> shift
    return jnp.where(r == c, 1.0, 0.0).astype(BF16)


def _gsum(x, ones):
    hi, lo = _split2(x)
    return jnp.dot(hi, ones, preferred_element_type=F32) + jnp.dot(lo, ones, preferred_element_type=F32)


def _lane_head(width, shift):
    return lax.broadcasted_iota(jnp.int32, (1, width), 1) >> shift


def _softplus(x):
    return jnp.maximum(x, 0.0) + jnp.log1p(jnp.exp(-jnp.abs(x)))


def _log_sigmoid(x):
    return jnp.minimum(x, 0.0) - jnp.log1p(jnp.exp(-jnp.abs(x)))


def _sigmoid(x):
    return 1.0 / (1.0 + jnp.exp(-x))


def _layernorm(x, g, b):
    mu = jnp.mean(x, axis=-1, keepdims=True)
    d = x - mu
    var = jnp.mean(d * d, axis=-1, keepdims=True)
    return d * lax.rsqrt(var + LN_EPS) * g + b


def _seg_cumsum(x, seg):
    pos = lax.broadcasted_iota(jnp.int32, x.shape, 0) & (seg - 1)
    k = 1
    while k < seg:
        x = x + jnp.where(pos >= k, pltpu.roll(x, k, 0), 0.0)
        k *= 2
    return x


def _stack_heads(x, lane_head):
    return jnp.concatenate([jnp.where(lane_head == h, x, 0.0) for h in range(N_HEADS)], axis=0)


def _unstack_heads(x, c):
    return (x[0:c] + x[c:2 * c]) + (x[2 * c:3 * c] + x[3 * c:4 * c])


def _inproj_kernel(x_ref, w_ref, o_ref):
    o_ref[...] = jnp.dot(x_ref[...].astype(BF16), w_ref[...], preferred_element_type=F32)


def _inproj(x, w_pack, layer):
    t = x.shape[0]
    return pl.pallas_call(
        _inproj_kernel,
        grid=(t // ROW_TILE,),
        in_specs=[pl.BlockSpec((ROW_TILE, D_MODEL), lambda i: (i, 0)),
                  pl.BlockSpec((None, D_MODEL, P_PACK), lambda i: (layer, 0, 0))],
        out_specs=pl.BlockSpec((ROW_TILE, P_PACK), lambda i: (i, 0)),
        out_shape=jax.ShapeDtypeStruct((t, P_PACK), F32),
        compiler_params=_cparams("parallel"),
        name="inproj",
    )(x, w_pack)


def _rows(ref):
    return ref[...].reshape(MIX_ROWS, ref.shape[-1])


def _sg_steps(u_ref, v_ref, g_ref, b_ref, w_ref, sb_ref, o_ref):
    ones = _group_ones(GROUP, 6)
    lane_head = _lane_head(GROUP, 6)
    v = _rows(v_ref)
    mean = _gsum(v, ones) * (1.0 / HEAD_DIM)
    d = v - mean
    var = _gsum(d * d, ones) * (1.0 / HEAD_DIM)
    vn = d * lax.rsqrt(var + LN_EPS) * g_ref[...] + b_ref[...]
    row = lax.broadcasted_iota(jnp.int32, (SG_CHUNK, SG_CHUNK), 0)
    col = lax.broadcasted_iota(jnp.int32, (SG_CHUNK, SG_CHUNK), 1)
    wm = [jnp.where(col <= row, w_ref[h], 0.0).astype(BF16) for h in range(N_HEADS)]
    yield
    per_seq = ROW_TILE // SG_CHUNK
    for c in range(MIX_ROWS // SG_CHUNK):
        vc = vn[c * SG_CHUNK:(c + 1) * SG_CHUNK]
        s = sb_ref[...]
        for h in range(N_HEADS):
            s = s + jnp.dot(wm[h], jnp.where(lane_head == h, vc, 0.0).astype(BF16),
                            preferred_element_type=F32)
        sl = slice((c % per_seq) * SG_CHUNK, (c % per_seq + 1) * SG_CHUNK)
        o_ref[c // per_seq, sl, :] = u_ref[c // per_seq, sl, :] * s
        yield


def _tri_inverse(mats, same_head, wt, ws):
    expand = lambda w: jnp.where(same_head, jnp.concatenate([w] * N_HEADS, axis=0), 0.0).astype(BF16)
    b16 = (wt >> 4) == (ws >> 4)
    b32 = (wt >> 5) == (ws >> 5)
    in32 = b32 & jnp.logical_not(b16)
    in64 = jnp.logical_not(b32)
    x = [jnp.where(b16, a, 0.0) for a in mats]
    t = [jnp.where(wt == ws, 1.0, 0.0) + d for d in x]
    for _ in range(3):
        x = [_dot(d, expand(d)) for d in x]
        yield True
        t = [ti + _dot(ti, expand(d)) for ti, d in zip(t, x)]
    for level in (in32, in64):
        yield True
        te = [_dot(ti, expand(jnp.where(level, a, 0.0))) for ti, a in zip(t, mats)]
        yield True
        t = [ti + _dot(tei, expand(ti)) for ti, tei in zip(t, te)]
    return t


def _rwkv_steps(r_ref, k_ref, v_ref, lo_ref, mu_ref, w0_ref, a0_ref, w2_ref, a2_ref, g2_ref,
                kk_ref, ka_ref, rk_ref, lng_ref, lnb_ref, o_ref,
                prev_ref, st_ref, at_s, bt_s, kt_s, rt_s, v_s, rf_s, y_s, g_s, n_s, h0_s):
    c_len = RW_CHUNK
    per_seq = ROW_TILE // c_len
    n_chunks = MIX_SEQS * per_seq
    ones = _group_ones(GROUP, 6)
    lane_head = _lane_head(GROUP, 6)
    row8 = lax.broadcasted_iota(jnp.int32, (8, GROUP), 0)

    def shifted(ref, j):
        z = _rows(ref)
        zs = pltpu.roll(z, 1, 0)
        pieces = []
        for s in range(MIX_SEQS):
            lo_row = s * ROW_TILE
            prev = prev_ref[j, s]
            pieces += [jnp.where(row8 == 0, prev[7:8, :], zs[lo_row:lo_row + 8]),
                       zs[lo_row + 8:lo_row + ROW_TILE]]
            prev_ref[j, s] = z[lo_row + ROW_TILE - 8:lo_row + ROW_TILE, :]
        return z + mu_ref[j] * (jnp.concatenate(pieces, axis=0) - z)

    r = shifted(r_ref, 0)
    k = shifted(k_ref, 1)
    v = shifted(v_ref, 2)
    lo = shifted(lo_ref, 3)

    wlog = -_softplus(-(w0_ref[...] + _dot(jnp.tanh(lo), w2_ref[...]))) - 0.5
    ld = -jnp.exp(wlog)
    a = _sigmoid(a0_ref[...] + _dot(lo, a2_ref[...]))
    gate = _dot(_sigmoid(lo), g2_ref[...])
    kk = k * kk_ref[...]
    kk = kk / jnp.maximum(jnp.sqrt(_gsum(kk * kk, ones)), 1e-12)
    k = k * (1.0 + (a - 1.0) * ka_ref[...])
    bonus = _gsum(r * k * rk_ref[...], ones) * v

    cl = _seg_cumsum(ld, c_len)
    e_pos = jnp.exp(cl)
    e_neg = jnp.exp(-cl)
    at_s[...] = (-kk * jnp.exp(cl - ld)).astype(BF16)
    bt_s[...] = (kk * a * e_neg).astype(BF16)
    kt_s[...] = (k * e_neg).astype(BF16)
    rt_s[...] = (r * e_pos).astype(BF16)
    v_s[...] = v.astype(BF16)
    w_chunk = [e_pos[(c + 1) * c_len - 1:(c + 1) * c_len, :] for c in range(n_chunks)]
    yield

    n4 = N_HEADS * c_len
    ri = lax.broadcasted_iota(jnp.int32, (n4, n4), 0)
    ci = lax.broadcasted_iota(jnp.int32, (n4, n4), 1)
    same_head = (ri >> 6) == (ci >> 6)
    wt = lax.broadcasted_iota(jnp.int32, (c_len, n4), 0)
    ws = lax.broadcasted_iota(jnp.int32, (c_len, n4), 1) & (c_len - 1)
    strict_w = ws < wt
    incl_w = ws <= wt

    chunks = range(n_chunks)
    rows = [slice(c * c_len, (c + 1) * c_len) for c in chunks]
    gram = [_dot_nt(jnp.concatenate([at_s[sl, :], rt_s[sl, :]], axis=0),
                    jnp.concatenate([_stack_heads(bt_s[sl, :], lane_head),
                                     _stack_heads(kt_s[sl, :], lane_head)], axis=0)) for sl in rows]
    a_ab = [jnp.where(strict_w, g[0:c_len, 0:n4], 0.0) for g in gram]
    yield
    t_wide = yield from _tri_inverse(a_ab, same_head, wt, ws)
    yield
    for c, sl in enumerate(rows):
        g = gram[c]
        a_ak = jnp.where(strict_w, g[0:c_len, n4:2 * n4], 0.0)
        l_rb = jnp.where(incl_w, g[c_len:2 * c_len, 0:n4], 0.0)
        l_rk = jnp.where(incl_w, g[c_len:2 * c_len, n4:2 * n4], 0.0)
        t_w = t_wide[c]
        v_stack = _stack_heads(v_s[sl, :], lane_head)
        ta = _dot(t_w, _stack_heads(at_s[sl, :], lane_head))
        uv = _dot(t_w, _stack_heads(_dot(a_ak, v_stack), lane_head))
        rf_s[sl, :] = (rt_s[sl, :] + _dot(l_rb, _stack_heads(ta, lane_head))).astype(BF16)
        y_s[sl, :] = _dot(l_rb, _stack_heads(uv, lane_head)) + _dot(l_rk, v_stack)
        bt = bt_s[sl, :]
        g_s[c] = (jnp.where(same_head, _dot_tn(ta, bt), 0.0) * w_chunk[c]).astype(BF16)
        n_s[c] = jnp.where(same_head, _dot_tn(jnp.concatenate([uv.astype(BF16), v_s[sl, :]], axis=0),
                                              jnp.concatenate([bt, kt_s[sl, :]], axis=0)),
                           0.0) * w_chunk[c]
        if c % 2 == 1:
            yield

    for step in range(per_seq):
        for s in range(MIX_SEQS):
            c = s * per_seq + step
            st = st_ref[s]
            stb = st.astype(BF16)
            h0_s[c] = stb
            st_ref[s] = st * w_chunk[c] + jnp.dot(stb, g_s[c], preferred_element_type=F32) + n_s[c]
        yield True

    for c, sl in enumerate(rows):
        y_s[sl, :] = y_s[sl, :] + _dot_nt(rf_s[sl, :], h0_s[c])

    y = y_s[...]
    mean = _gsum(y, ones) * (1.0 / HEAD_DIM)
    d = y - mean
    var = _gsum(d * d, ones) * (1.0 / HEAD_DIM)
    y = d * lax.rsqrt(var + RW_LNX_EPS) * lng_ref[...] + lnb_ref[...]
    o_ref[...] = ((y + bonus) * gate).reshape(MIX_SEQS, ROW_TILE, GROUP)


def _gla_steps(qk_ref, v_ref, g_ref, m_ref, aup_ref, ab_ref, ng_ref, o_ref,
               st_ref, qe_s, ke_s, k2_s, qi_s, v_s, o_s):
    c_len = GLA_CHUNK
    per_seq = ROW_TILE // c_len
    n_chunks = MIX_SEQS * per_seq
    kw = N_HEADS * GLA_DK
    lg = _log_sigmoid(_dot(_rows(m_ref), aup_ref[...]) + ab_ref[...]) * (1.0 / GLA_TAU)
    b = _seg_cumsum(lg, c_len).reshape(n_chunks, c_len, kw)
    b_mid = b[:, c_len // 2:c_len // 2 + 1, :]
    b_last = b[:, c_len - 1:c_len, :]
    qk = _rows(qk_ref)
    q = (qk[:, 0:kw] * (GLA_DK ** -0.5)).reshape(n_chunks, c_len, kw)
    k = qk[:, kw:2 * kw].reshape(n_chunks, c_len, kw)
    qe_s[...] = (q * jnp.exp(b - b_mid)).reshape(MIX_ROWS, kw).astype(BF16)
    ke_s[...] = (k * jnp.exp(b_mid - b)).reshape(MIX_ROWS, kw).astype(BF16)
    k2_s[...] = (k * jnp.exp(b_last - b)).reshape(MIX_ROWS, kw).astype(BF16)
    qi_s[...] = (q * jnp.exp(b)).reshape(MIX_ROWS, kw).astype(BF16)
    v_s[...] = _rows(v_ref).astype(BF16)
    decay = [jnp.exp(b_last[c]) for c in range(n_chunks)]
    yield

    k_head = _lane_head(kw, 5)
    v_head = _lane_head(GROUP, 6)
    causal = ((lax.broadcasted_iota(jnp.int32, (c_len, N_HEADS * c_len), 1) & (c_len - 1))
              <= lax.broadcasted_iota(jnp.int32, (c_len, N_HEADS * c_len), 0))
    same_head = ((lax.broadcasted_iota(jnp.int32, (GROUP, kw), 0) >> 6)
                 == (lax.broadcasted_iota(jnp.int32, (GROUP, kw), 1) >> 5))

    rows = [slice(c * c_len, (c + 1) * c_len) for c in range(n_chunks)]
    inc = [jnp.where(same_head, _dot_tn(v_s[sl, :], k2_s[sl, :]), 0.0) for sl in rows]
    yield
    entering = []
    for s in range(MIX_SEQS):
        st = st_ref[s]
        for c in range(s * per_seq, (s + 1) * per_seq):
            entering.append(st.astype(BF16))
            st = st * decay[c] + inc[c]
        st_ref[s] = st
    yield
    for c, sl in enumerate(rows):
        att = jnp.where(causal, _dot_nt(qe_s[sl, :], _stack_heads(ke_s[sl, :], k_head)), 0.0)
        o_s[sl, :] = _dot(att, _stack_heads(v_s[sl, :], v_head)) + _dot_nt(qi_s[sl, :], entering[c])
        if c % 2 == 1:
            yield

    o = o_s[...]
    ms = _gsum(o * o, _group_ones(GROUP, 6)) * (1.0 / HEAD_DIM)
    g = _rows(g_ref)
    o_ref[...] = (o * lax.rsqrt(ms + GLA_NORM_EPS) * ng_ref[...]
                  * (g * _sigmoid(g))).reshape(MIX_SEQS, ROW_TILE, GROUP)


N_RW_CHUNKS = MIX_ROWS // RW_CHUNK
_KEY_LANES = N_HEADS * GLA_DK
RW_SCRATCH = ([pltpu.VMEM((4, MIX_SEQS, 8, GROUP), F32), pltpu.VMEM((MIX_SEQS, GROUP, GROUP), F32)]
              + [pltpu.VMEM((MIX_ROWS, GROUP), BF16)] * 6 + [pltpu.VMEM((MIX_ROWS, GROUP), F32)]
              + [pltpu.VMEM((N_RW_CHUNKS, GROUP, GROUP), BF16), pltpu.VMEM((N_RW_CHUNKS, GROUP, GROUP), F32),
                 pltpu.VMEM((N_RW_CHUNKS, GROUP, GROUP), BF16)])
GLA_SCRATCH = ([pltpu.VMEM((MIX_SEQS, GROUP, _KEY_LANES), F32)] + [pltpu.VMEM((MIX_ROWS, _KEY_LANES), BF16)] * 4
               + [pltpu.VMEM((MIX_ROWS, GROUP), BF16), pltpu.VMEM((MIX_ROWS, GROUP), F32)])


def _mixers_kernel(*refs):
    it = iter(refs)
    take = lambda n: [next(it) for _ in range(n)]
    sg_in, rw_in, gla_in = take(6), take(15), take(7)
    sg_out, rw_out, gla_out = take(1), take(1), take(1)
    rw_scr, gla_scr = take(len(RW_SCRATCH)), take(len(GLA_SCRATCH))

    @pl.when(pl.program_id(1) == 0)
    def _():
        for carried in (rw_scr[0], rw_scr[1], gla_scr[0]):
            carried[...] = jnp.zeros_like(carried)

    rwkv = _rwkv_steps(*rw_in, *rw_out, *rw_scr)
    filler = itertools.chain(_gla_steps(*gla_in, *gla_out, *gla_scr), _sg_steps(*sg_in, *sg_out))
    for latency_bound in rwkv:
        if latency_bound:
            next(filler, None)
    for _ in filler:
        pass


def _token_mixers(z, p, layer, batch, seq):
    assert batch % MIX_SEQS == 0
    z = z.reshape(batch, seq, P_PACK)
    kw = N_HEADS * GLA_DK
    row = lambda c: pl.BlockSpec((MIX_SEQS, ROW_TILE, GROUP), lambda b, i, c=c: (b, i, c))
    par = lambda *shape: pl.BlockSpec((None,) + shape, lambda b, i: (layer,) + (0,) * len(shape))
    vec, mat = par(1, GROUP), par(GROUP, GROUP)
    out = pl.BlockSpec((MIX_SEQS, ROW_TILE, GROUP), lambda b, i: (b, i, 0))
    y = jax.ShapeDtypeStruct((batch, seq, GROUP), F32)
    ys = pl.pallas_call(
        _mixers_kernel,
        grid=(batch // MIX_SEQS, seq // ROW_TILE),
        in_specs=[row(0), row(1), vec, vec, par(N_HEADS, SG_CHUNK, SG_CHUNK), par(SG_CHUNK, GROUP),
                  row(2), row(3), row(4), row(5), par(4, 1, GROUP),
                  vec, vec, mat, mat, mat, vec, vec, vec, vec, vec,
                  row(6), row(7), row(8),
                  pl.BlockSpec((MIX_SEQS, ROW_TILE, 128), lambda b, i: (b, i, MISC_BLOCK)),
                  par(128, kw), par(1, kw), vec],
        out_specs=[out, out, out],
        out_shape=[y, y, y],
        scratch_shapes=RW_SCRATCH + GLA_SCRATCH,
        compiler_params=_cparams("parallel", "arbitrary"),
        name="token_mixers",
    )(z, z, p["sg_ln_g"], p["sg_ln_b"], p["sg_w"], p["sg_bias"],
      z, z, z, z, p["rw_mu"], p["rw_w0"], p["rw_a0"], p["rw_w2"], p["rw_a2"], p["rw_g2"],
      p["rw_kk"], p["rw_ka"], p["rw_rk"], p["rw_lnx_g"], p["rw_lnx_b"],
      z, z, z, z, p["gla_a_up"], p["gla_a_b"], p["gla_norm_g"])
    return [a.reshape(batch * seq, GROUP) for a in ys]


def _fox_prep_kernel(q_ref, k_ref, v_ref, z_ref, fb_ref, qa_ref, ka_ref, vb_ref):
    c = _seg_cumsum(_log_sigmoid(z_ref[...] + fb_ref[...]), z_ref.shape[0])
    lane = lax.broadcasted_iota(jnp.int32, (1, 128), 1)
    for h in range(N_HEADS):
        ch = c[:, MISC_FOX_LANE + h:MISC_FOX_LANE + h + 1]
        c1 = ch.astype(BF16).astype(F32)
        c2 = (ch - c1).astype(BF16).astype(F32)
        c3 = ch - c1 - c2
        tile = slice((h // 2) * 128, (h // 2 + 1) * 128)
        off = (h % 2) * HEAD_DIM
        aug = HEAD_DIM - off
        in_head = (lane >> 6) == (h % 2)
        qa = jnp.where((lane >= aug + 3) & (lane < aug + 6), 1.0, 0.0)
        ka = jnp.where((lane >= aug) & (lane < aug + 3), 1.0, 0.0)
        for n, cn in enumerate((c1, c2, c3)):
            qa = jnp.where(lane == aug + n, cn, qa)
            ka = jnp.where(lane == aug + 3 + n, -cn, ka)
        qa = jnp.where(in_head, q_ref[:, tile] * (HEAD_DIM ** -0.5), qa)
        ka = jnp.where(in_head, k_ref[:, tile], ka)
        qa_ref[h] = qa.astype(BF16)
        ka_ref[h] = ka.astype(BF16)
    vb_ref[...] = v_ref[...].astype(BF16)


def _fox_prep(z, fb, layer, batch, seq):
    t = z.shape[0]
    col = lambda c: pl.BlockSpec((seq, GROUP), lambda b, c=c: (b, c))
    head = pl.BlockSpec((N_HEADS, seq, 128), lambda b: (0, b, 0))
    return pl.pallas_call(
        _fox_prep_kernel,
        grid=(batch,),
        in_specs=[col(9), col(10), col(11),
                  pl.BlockSpec((seq, 128), lambda b: (b, MISC_BLOCK)),
                  pl.BlockSpec((None, 1, 128), lambda b: (layer, 0, 0))],
        out_specs=[head, head, pl.BlockSpec((seq, GROUP), lambda b: (b, 0))],
        out_shape=[jax.ShapeDtypeStruct((N_HEADS, t, 128), BF16),
                   jax.ShapeDtypeStruct((N_HEADS, t, 128), BF16),
                   jax.ShapeDtypeStruct((t, GROUP), BF16)],
        compiler_params=_cparams("parallel"),
        name="fox_prep",
    )(z, z, z, z, fb)


def _fox_kernel(qi_ref, kj_ref, qa_ref, ka_ref, v_ref, o_ref, m_ref, l_ref, acc_ref):
    tq = ROW_TILE
    i = qi_ref[pl.program_id(1)]
    j = kj_ref[pl.program_id(1)]

    @pl.when(j == 0)
    def _():
        m_ref[...] = jnp.full_like(m_ref, -1e30)
        l_ref[...] = jnp.zeros_like(l_ref)
        acc_ref[...] = jnp.zeros_like(acc_ref)

    lane_head = _lane_head(GROUP, 6)

    def update(on_diagonal):
        vb = v_ref[...]
        acc = acc_ref[...]
        if on_diagonal:
            causal = (lax.broadcasted_iota(jnp.int32, (tq, tq), 1)
                      <= lax.broadcasted_iota(jnp.int32, (tq, tq), 0))
        heads = range(N_HEADS)
        s = [lax.dot_general(qa_ref[h], ka_ref[h], (((1,), (1,)), ((), ())),
                             preferred_element_type=F32) for h in heads]
        if on_diagonal:
            s = [jnp.where(causal, s[h], -1e30) for h in heads]
        m_prev = [m_ref[h] for h in heads]
        m_new = [jnp.maximum(m_prev[h], jnp.max(s[h], axis=-1, keepdims=True)) for h in heads]
        alpha = [jnp.exp(m_prev[h] - m_new[h]) for h in heads]
        p = [jnp.exp(s[h] - jnp.concatenate([m_new[h]] * (tq // 128), axis=1)) for h in heads]
        for h in heads:
            l_ref[h] = alpha[h] * l_ref[h] + jnp.sum(p[h], axis=-1, keepdims=True)
            m_ref[h] = m_new[h]
        pv = [jnp.dot(p[h].astype(BF16), vb, preferred_element_type=F32) for h in heads]
        pv = jnp.where(lane_head < 2, jnp.where(lane_head == 0, pv[0], pv[1]),
                       jnp.where(lane_head == 2, pv[2], pv[3]))
        return acc * per_head(alpha) + pv

    first_half = lax.broadcasted_iota(jnp.int32, (1, 128), 1) < HEAD_DIM

    def per_head(xs):
        return jnp.concatenate([jnp.where(first_half, xs[0], xs[1]),
                                jnp.where(first_half, xs[2], xs[3])], axis=1)

    @pl.when(j < i)
    def _():
        acc_ref[...] = update(False)

    @pl.when(j == i)
    def _():
        acc = update(True)
        o_ref[...] = acc * per_head([1.0 / l_ref[h] for h in range(N_HEADS)])


def _fox(qa, ka, vb, batch, seq):
    t = vb.shape[0]
    tiles = seq // ROW_TILE
    pairs = [(i, j) for i in range(tiles) for j in range(i + 1)]
    q_tile = jnp.asarray([i for i, _ in pairs], jnp.int32)
    k_tile = jnp.asarray([j for _, j in pairs], jnp.int32)
    return pl.pallas_call(
        _fox_kernel,
        grid_spec=pltpu.PrefetchScalarGridSpec(
            num_scalar_prefetch=2,
            grid=(batch, len(pairs)),
            in_specs=[pl.BlockSpec((N_HEADS, ROW_TILE, 128),
                                   lambda b, t, qi, kj: (0, b * tiles + qi[t], 0)),
                      pl.BlockSpec((N_HEADS, ROW_TILE, 128),
                                   lambda b, t, qi, kj: (0, b * tiles + kj[t], 0)),
                      pl.BlockSpec((ROW_TILE, GROUP), lambda b, t, qi, kj: (b * tiles + kj[t], 0))],
            out_specs=pl.BlockSpec((ROW_TILE, GROUP), lambda b, t, qi, kj: (b * tiles + qi[t], 0)),
            scratch_shapes=[pltpu.VMEM((N_HEADS, ROW_TILE, 128), F32),
                            pltpu.VMEM((N_HEADS, ROW_TILE, 128), F32),
                            pltpu.VMEM((ROW_TILE, GROUP), F32)]),
        out_shape=jax.ShapeDtypeStruct((t, GROUP), F32),
        compiler_params=_cparams("parallel", "arbitrary"),
        name="fox_attention",
    )(q_tile, k_tile, qa, ka, vb)


def _memkv_kernel(mem_ref, g_ref, b_ref, w_ref, o_ref):
    memn = _layernorm(mem_ref[...], g_ref[...], b_ref[...])
    o_ref[...] = _dot(memn, w_ref[...]).astype(BF16)


def _memkv(mem, g, b, w_kv):
    rows = mem.shape[0]
    n = w_kv.shape[0]
    return pl.pallas_call(
        _memkv_kernel,
        grid=(n,),
        in_specs=[pl.BlockSpec((rows, D_MODEL), lambda i: (0, 0)),
                  pl.BlockSpec((1, D_MODEL), lambda i: (0, 0)),
                  pl.BlockSpec((1, D_MODEL), lambda i: (0, 0)),
                  pl.BlockSpec((None, D_MODEL, D_MODEL), lambda i: (i, 0, 0))],
        out_specs=pl.BlockSpec((None, rows, D_MODEL), lambda i: (i, 0, 0)),
        out_shape=jax.ShapeDtypeStruct((n, rows, D_MODEL), BF16),
        compiler_params=_cparams("parallel"),
        name="memory_kv",
    )(mem, g, b, w_kv)


def _mix_cross_kernel(ya_ref, yb_ref, yc_ref, yd_ref, wm_ref, x_ref, g1_ref, b1_ref,
                      k_ref, v_ref, wq_ref, wo_ref, g_ref, b_ref, o_ref):
    y = jnp.concatenate([r[...].astype(BF16) for r in (ya_ref, yb_ref, yc_ref, yd_ref)], axis=1)
    mix = jnp.dot(y, wm_ref[...], preferred_element_type=F32)
    x = _layernorm(DN_ALPHA * x_ref[...] + mix, g1_ref[...], b1_ref[...])
    q = jnp.dot(x.astype(BF16), wq_ref[...], preferred_element_type=F32).astype(BF16)
    heads = range(CA_HEADS)
    cols = [slice(h * CA_HEAD_DIM, (h + 1) * CA_HEAD_DIM) for h in heads]
    s = [_dot_nt(q[:, cols[h]], k_ref[:, cols[h]]) * (CA_HEAD_DIM ** -0.5) for h in heads]
    p = [jnp.exp(s[h] - jnp.max(s[h], axis=-1, keepdims=True)) for h in heads]
    p = [p[h] * (1.0 / jnp.sum(p[h], axis=-1, keepdims=True)) for h in heads]
    o = [jnp.dot(p[h].astype(BF16), v_ref[:, cols[h]], preferred_element_type=F32).astype(BF16)
         for h in heads]
    ca = jnp.dot(jnp.concatenate(o, axis=1), wo_ref[...], preferred_element_type=F32)
    o_ref[...] = _layernorm(DN_ALPHA * x + ca, g_ref[...], b_ref[...])


def _mix_cross(ys, x, kv, p, layer, batch, seq):
    t = x.shape[0]
    tiles = seq // ROW_TILE
    vec = pl.BlockSpec((None, 1, D_MODEL), lambda bb, i: (layer, 0, 0))
    mat = pl.BlockSpec((None, D_MODEL, D_MODEL), lambda bb, i: (layer, 0, 0))
    y_spec = pl.BlockSpec((ROW_TILE, GROUP), lambda bb, i: (bb * tiles + i, 0))
    x_spec = pl.BlockSpec((ROW_TILE, D_MODEL), lambda bb, i: (bb * tiles + i, 0))
    return pl.pallas_call(
        _mix_cross_kernel,
        grid=(batch, tiles),
        in_specs=[y_spec, y_spec, y_spec, y_spec, mat, x_spec, vec, vec,
                  pl.BlockSpec((None, N_MEM, D_MODEL), lambda bb, i: (2 * layer, bb, 0)),
                  pl.BlockSpec((None, N_MEM, D_MODEL), lambda bb, i: (2 * layer + 1, bb, 0)),
                  mat, mat, vec, vec],
        out_specs=x_spec,
        out_shape=jax.ShapeDtypeStruct((t, D_MODEL), F32),
        compiler_params=_cparams("parallel", "parallel"),
        name="mix_cross_attention",
    )(*ys, p["w_out"], x, p["ln1_g"], p["ln1_b"], kv, kv, p["ca_wq"], p["ca_wo"],
      p["ln2_g"], p["ln2_b"])


def _ffn_kernel(x_ref, wu_ref, bu_ref, cw_ref, cb_ref, wd_ref, g_ref, b_ref, o_ref,
                act_ref, carry_ref, *, tiles_per_seq):
    tm = ROW_TILE
    tf = FFN_COL_TILE
    first = (pl.program_id(0) % tiles_per_seq) == 0

    @pl.when(first)
    def _():
        carry_ref[...] = jnp.zeros_like(carry_ref)

    x = x_ref[...]
    xb = x.astype(BF16)
    row8 = lax.broadcasted_iota(jnp.int32, (8, tf), 0)

    def conv(col):
        cols = slice(col, col + tf)
        h = jnp.dot(xb, wu_ref[:, cols], preferred_element_type=F32) + bu_ref[:, cols]
        prev = carry_ref[:, cols]
        carry_ref[:, cols] = h[tm - 8:tm, :]
        h1 = pltpu.roll(h, 1, 0)
        h2 = pltpu.roll(h, 2, 0)
        top1 = jnp.where(row8 == 0, prev[7:8, :], h1[0:8, :])
        top2 = jnp.where(row8 == 0, prev[6:7, :], jnp.where(row8 == 1, prev[7:8, :], h2[0:8, :]))
        h1 = jnp.concatenate([top1, h1[8:tm, :]], axis=0)
        h2 = jnp.concatenate([top2, h2[8:tm, :]], axis=0)
        cw = cw_ref[:, cols]
        return cw[0:1, :] * h2 + cw[1:2, :] * h1 + cw[2:3, :] * h + cb_ref[:, cols]

    for c in range(D_FF // tf):
        gate = conv(c * tf)
        val = conv(D_FF + c * tf)
        act = 0.5 * gate * (1.0 + lax.erf(gate * (2.0 ** -0.5))) * val
        act_ref[:, c * tf:(c + 1) * tf] = act.astype(BF16)

    ff = jnp.dot(act_ref[...], wd_ref[...], preferred_element_type=F32)
    o_ref[...] = _layernorm(DN_ALPHA * x + ff, g_ref[...], b_ref[...])


def _ffn(x, p, layer, seq):
    t = x.shape[0]
    once = pl.Buffered(1)
    vec = pl.BlockSpec((None, 1, D_MODEL), lambda i: (layer, 0, 0))
    wide = lambda rows: pl.BlockSpec((None, rows, 2 * D_FF), lambda i: (layer, 0, 0))
    return pl.pallas_call(
        functools.partial(_ffn_kernel, tiles_per_seq=seq // ROW_TILE),
        grid=(t // ROW_TILE,),
        in_specs=[pl.BlockSpec((ROW_TILE, D_MODEL), lambda i: (i, 0)),
                  pl.BlockSpec((None, D_MODEL, 2 * D_FF), lambda i: (layer, 0, 0), pipeline_mode=once),
                  wide(1), wide(3), wide(1),
                  pl.BlockSpec((None, D_FF, D_MODEL), lambda i: (layer, 0, 0), pipeline_mode=once),
                  vec, vec],
        out_specs=pl.BlockSpec((ROW_TILE, D_MODEL), lambda i: (i, 0)),
        out_shape=jax.ShapeDtypeStruct((t, D_MODEL), F32),
        scratch_shapes=[pltpu.VMEM((ROW_TILE, D_FF), BF16), pltpu.VMEM((8, 2 * D_FF), F32)],
        compiler_params=_cparams("arbitrary"),
        name="conv_ffn_ln",
    )(x, p["ffn_up"], p["ffn_up_b"], p["ffn_conv"], p["ffn_conv_b"], p["ffn_down"],
      p["ln3_g"], p["ln3_b"])


def _pack_params(w_in, w_out, sg_ln_g, sg_ln_b, sg_w, sg_b, rw_mu, rw_w0, rw_w2, rw_a0, rw_a2, rw_g2,
                 rw_kk, rw_ka, rw_rk, rw_lnx_g, rw_lnx_b, gla_a_up, gla_a_b, gla_norm_g, fox_fb,
                 ln1_g, ln1_b, ca_wq, ca_wk, ca_wv, ca_wo, ln2_g, ln2_b,
                 ffn_up, ffn_up_b, ffn_conv, ffn_conv_b, ffn_down, ln3_g, ln3_b):
    depth = w_in.shape[0]
    row = lambda a: a.reshape(depth, 1, -1)
    gla0, fox0 = 1536, 2320
    misc = jnp.zeros((depth, D_MODEL, 128), F32)
    misc = misc.at[:, :, 0:16].set(w_in[:, :, gla0 + 768:gla0 + 784])
    misc = misc.at[:, :, MISC_FOX_LANE:MISC_FOX_LANE + 4].set(w_in[:, :, fox0 + 768:fox0 + 772])
    w_pack = jnp.concatenate([w_in[:, :, 0:gla0 + 768], w_in[:, :, fox0:fox0 + 768], misc], axis=-1)

    def pad_rows(w, start):
        out = jnp.zeros((depth, GROUP, GROUP), F32)
        return out.at[:, start:start + w.shape[1], :].set(w).astype(BF16)

    fb = jnp.zeros((depth, 1, 128), F32).at[:, 0, MISC_FOX_LANE:MISC_FOX_LANE + 4].set(fox_fb)
    aup = jnp.zeros((depth, 128, N_HEADS * GLA_DK), F32).at[:, 0:16, :].set(gla_a_up)
    return {
        "w_in": w_pack.astype(BF16),
        "w_out": w_out.astype(BF16),
        "sg_ln_g": row(sg_ln_g), "sg_ln_b": row(sg_ln_b), "sg_w": sg_w,
        "sg_bias": jnp.repeat(jnp.swapaxes(sg_b, 1, 2), HEAD_DIM, axis=2),
        "rw_mu": rw_mu.reshape(depth, 4, 1, GROUP),
        "rw_w0": row(rw_w0), "rw_a0": row(rw_a0),
        "rw_w2": pad_rows(rw_w2, 0), "rw_a2": pad_rows(rw_a2, 64), "rw_g2": pad_rows(rw_g2, 128),
        "rw_kk": row(rw_kk), "rw_ka": row(rw_ka), "rw_rk": row(rw_rk),
        "rw_lnx_g": row(rw_lnx_g), "rw_lnx_b": row(rw_lnx_b),
        "gla_a_up": aup.astype(BF16), "gla_a_b": row(gla_a_b), "gla_norm_g": row(gla_norm_g),
        "fox_fb": fb,
        "ln1_g": row(ln1_g), "ln1_b": row(ln1_b),
        "ca_wq": ca_wq.astype(BF16), "ca_wo": ca_wo.astype(BF16),
        "ca_wkv": jnp.stack([ca_wk, ca_wv], axis=1).reshape(2 * depth, D_MODEL, D_MODEL).astype(BF16),
        "ln2_g": row(ln2_g), "ln2_b": row(ln2_b),
        "ffn_up": ffn_up.astype(BF16), "ffn_up_b": row(ffn_up_b), "ffn_conv": ffn_conv,
        "ffn_conv_b": row(ffn_conv_b), "ffn_down": ffn_down.astype(BF16),
        "ln3_g": row(ln3_g), "ln3_b": row(ln3_b),
    }


def _mixer_outputs(x, p, layer, batch, seq):
    z = _inproj(x, p["w_in"], layer)
    y_a, y_b, y_c = _token_mixers(z, p, layer, batch, seq)
    y_d = _fox(*_fox_prep(z, p["fox_fb"], layer, batch, seq), batch, seq)
    return y_a, y_b, y_c, y_d


def kernel(x, mem, mem_ln_g, mem_ln_b, w_in, w_out, sg_ln_g, sg_ln_b, sg_w, sg_b, rw_mu, rw_w0, rw_w2, rw_a0, rw_a2, rw_g2, rw_kk, rw_ka, rw_rk, rw_lnx_g, rw_lnx_b, gla_a_up, gla_a_b, gla_norm_g, fox_fb, ln1_g, ln1_b, ca_wq, ca_wk, ca_wv, ca_wo, ln2_g, ln2_b, ffn_up, ffn_up_b, ffn_conv, ffn_conv_b, ffn_down, ln3_g, ln3_b):
    batch, seq, _ = x.shape
    depth = w_in.shape[0]
    p = _pack_params(w_in, w_out, sg_ln_g, sg_ln_b, sg_w, sg_b, rw_mu, rw_w0, rw_w2, rw_a0, rw_a2,
                     rw_g2, rw_kk, rw_ka, rw_rk, rw_lnx_g, rw_lnx_b, gla_a_up, gla_a_b, gla_norm_g,
                     fox_fb, ln1_g, ln1_b, ca_wq, ca_wk, ca_wv, ca_wo, ln2_g, ln2_b,
                     ffn_up, ffn_up_b, ffn_conv, ffn_conv_b, ffn_down, ln3_g, ln3_b)
    kv = _memkv(mem.reshape(batch * N_MEM, D_MODEL), mem_ln_g.reshape(1, -1), mem_ln_b.reshape(1, -1),
                p["ca_wkv"])
    h = x.reshape(batch * seq, D_MODEL)
    for layer in range(depth):
        ys = _mixer_outputs(h, p, layer, batch, seq)
        h = _mix_cross(ys, h, kv, p, layer, batch, seq)
        h = _ffn(h, p, layer, seq)
    return h.reshape(batch, seq, D_MODEL)
```

```python
import functools
import itertools

import jax
import jax.numpy as jnp
import numpy as np
from jax import lax
from jax.experimental import pallas as pl
from jax.experimental.pallas import tpu as pltpu

F32 = jnp.float32
BF16 = jnp.bfloat16

D_MODEL = 1024
DEPTH = 4
N_MEM = 256
HEAD_DIM = 64
N_HEADS = 4
GROUP = 256
SG_CHUNK = 128
RW_LNX_EPS = 64e-5
RW_CHUNK = 64
GLA_DK = 32
GLA_TAU = 16.0
GLA_CHUNK = 64
GLA_NORM_EPS = 1e-5
CA_HEADS = 4
CA_HEAD_DIM = 256
D_FF = 2816
DN_ALPHA = (2.0 * DEPTH) ** 0.25
LN_EPS = 1e-5
LOG2_E = 1.4426950408889634

P_PACK = 12 * 256 + 128
W_IN_GLA = 1536
W_IN_FOX = 2320
MISC_BLOCK = 24
MISC_FOX_LANE = 16

ROW_TILE = 512
MIX_SEQS = 2
MIX_ROWS = MIX_SEQS * ROW_TILE
FFN_COL_TILE = 256
VMEM_LIMIT = 56 * 1024 * 1024


def _cparams(*sem):
    return pltpu.CompilerParams(dimension_semantics=sem, vmem_limit_bytes=VMEM_LIMIT)


def _dot(a, b):
    return jnp.dot(a.astype(BF16), b.astype(BF16), preferred_element_type=F32)


def _dot_nt(a, b):
    return lax.dot_general(a.astype(BF16), b.astype(BF16), (((1,), (1,)), ((), ())),
                           preferred_element_type=F32)


def _dot_tn(a, b):
    return lax.dot_general(a.astype(BF16), b.astype(BF16), (((0,), (0,)), ((), ())),
                           preferred_element_type=F32)


def _split2(a):
    hi = a.astype(BF16)
    lo = (a - hi.astype(F32)).astype(BF16)
    return hi, lo


def _dot3(a, b):
    ah, al = _split2(a)
    bh, bl = _split2(b)
    return (jnp.dot(ah, bh, preferred_element_type=F32)
            + (jnp.dot(ah, bl, preferred_element_type=F32) + jnp.dot(al, bh, preferred_element_type=F32)))


def _group_ones(n, shift):
    r = lax.broadcasted_iota(jnp.int32, (n, n), 0) >> shift
    c = lax.broadcasted_iota(jnp.int32, (n, n), 1) >> shift
    return jnp.where(r == c, 1.0, 0.0).astype(BF16)


def _gsum(x, ones):
    hi, lo = _split2(x)
    return jnp.dot(hi, ones, preferred_element_type=F32) + jnp.dot(lo, ones, preferred_element_type=F32)


def _lane_head(width, shift):
    return lax.broadcasted_iota(jnp.int32, (1, width), 1) >> shift


def _softplus(x):
    return jnp.maximum(x, 0.0) + jnp.log1p(jnp.exp(-jnp.abs(x)))


def _log_sigmoid(x):
    return jnp.minimum(x, 0.0) - jnp.log1p(jnp.exp(-jnp.abs(x)))


def _sigmoid(x):
    return 1.0 / (1.0 + jnp.exp(-x))


def _layernorm(x, g, b):
    mu = jnp.mean(x, axis=-1, keepdims=True)
    d = x - mu
    var = jnp.mean(d * d, axis=-1, keepdims=True)
    return d * lax.rsqrt(var + LN_EPS) * g + b


def _seg_cumsum(x, seg):
    pos = lax.broadcasted_iota(jnp.int32, x.shape, 0) & (seg - 1)
    k = 1
    while k < seg:
        x = x + jnp.where(pos >= k, pltpu.roll(x, k, 0), 0.0)
        k *= 2
    return x


def _stack_heads(x, lane_head):
    return jnp.concatenate([jnp.where(lane_head == h, x, 0.0) for h in range(N_HEADS)], axis=0)


def _unstack_heads(x, c):
    return (x[0:c] + x[c:2 * c]) + (x[2 * c:3 * c] + x[3 * c:4 * c])


def _inproj_kernel(x_ref, w_ref, o_ref):
    z = jnp.dot(x_ref[...].astype(BF16), w_ref[...], preferred_element_type=F32)
    main = W_IN_GLA + 768
    o_ref[:, 0:main] = z[:, 0:main]
    o_ref[:, main:main + 768] = z[:, W_IN_FOX:W_IN_FOX + 768]
    tail = jnp.concatenate([z[:, main:main + 16], z[:, W_IN_FOX + 768:W_IN_FOX + 772],
                            jnp.zeros((ROW_TILE, 128 - 20), F32)], axis=1)
    o_ref[:, main + 768:P_PACK] = tail


def _inproj(x, w_in, layer):
    t = x.shape[0]
    p_in = w_in.shape[-1]
    return pl.pallas_call(
        _inproj_kernel,
        grid=(t // ROW_TILE,),
        in_specs=[pl.BlockSpec((ROW_TILE, D_MODEL), lambda i: (i, 0)),
                  pl.BlockSpec((None, D_MODEL, p_in), lambda i: (layer, 0, 0))],
        out_specs=pl.BlockSpec((ROW_TILE, P_PACK), lambda i: (i, 0)),
        out_shape=jax.ShapeDtypeStruct((t, P_PACK), F32),
        compiler_params=_cparams("parallel"),
        name="inproj",
    )(x, w_in)


def _rows(ref):
    return ref[...].reshape(MIX_ROWS, ref.shape[-1])


def _sg_steps(u_ref, v_ref, g_ref, b_ref, w_ref, sb_ref, o_ref):
    ones = _group_ones(GROUP, 6)
    lane_head = _lane_head(GROUP, 6)
    v = _rows(v_ref)
    mean = _gsum(v, ones) * (1.0 / HEAD_DIM)
    d = v - mean
    var = _gsum(d * d, ones) * (1.0 / HEAD_DIM)
    vn = d * lax.rsqrt(var + LN_EPS) * g_ref[...] + b_ref[...]
    row = lax.broadcasted_iota(jnp.int32, (SG_CHUNK, SG_CHUNK), 0)
    col = lax.broadcasted_iota(jnp.int32, (SG_CHUNK, SG_CHUNK), 1)
    wm = [jnp.where(col <= row, w_ref[h], 0.0).astype(BF16) for h in range(N_HEADS)]
    yield
    per_seq = ROW_TILE // SG_CHUNK
    for c in range(MIX_ROWS // SG_CHUNK):
        vc = vn[c * SG_CHUNK:(c + 1) * SG_CHUNK]
        s = sb_ref[...]
        for h in range(N_HEADS):
            s = s + jnp.dot(wm[h], jnp.where(lane_head == h, vc, 0.0).astype(BF16),
                            preferred_element_type=F32)
        sl = slice((c % per_seq) * SG_CHUNK, (c % per_seq + 1) * SG_CHUNK)
        o_ref[c // per_seq, sl, :] = u_ref[c // per_seq, sl, :] * s
        yield


def _tri_inverse(mats, same_head, wt, ws):
    expand = lambda w: jnp.where(same_head, jnp.concatenate([w] * N_HEADS, axis=0), 0.0).astype(BF16)
    b16 = (wt >> 4) == (ws >> 4)
    b32 = (wt >> 5) == (ws >> 5)
    in32 = b32 & jnp.logical_not(b16)
    in64 = jnp.logical_not(b32)
    x = [jnp.where(b16, a, 0.0) for a in mats]
    t = [jnp.where(wt == ws, 1.0, 0.0) + d for d in x]
    for _ in range(3):
        x = [_dot(d, expand(d)) for d in x]
        yield True
        t = [ti + _dot(ti, expand(d)) for ti, d in zip(t, x)]
    for level in (in32, in64):
        yield True
        te = [_dot(ti, expand(jnp.where(level, a, 0.0))) for ti, a in zip(t, mats)]
        yield True
        t = [ti + _dot(tei, expand(ti)) for ti, tei in zip(t, te)]
    return t


def _rwkv_steps(r_ref, k_ref, v_ref, lo_ref, mu_ref, w0_ref, a0_ref, w2_ref, a2_ref, g2_ref,
                kk_ref, ka_ref, rk_ref, lng_ref, lnb_ref, o_ref,
                prev_ref, st_ref, at_s, bt_s, kt_s, rt_s, v_s, rf_s, y_s, g_s, n_s, h0_s):
    c_len = RW_CHUNK
    per_seq = ROW_TILE // c_len
    n_chunks = MIX_SEQS * per_seq
    ones = _group_ones(GROUP, 6)
    lane_head = _lane_head(GROUP, 6)
    row8 = lax.broadcasted_iota(jnp.int32, (8, GROUP), 0)

    def shifted(ref, j):
        z = _rows(ref)
        zs = pltpu.roll(z, 1, 0)
        pieces = []
        for s in range(MIX_SEQS):
            lo_row = s * ROW_TILE
            prev = prev_ref[j, s]
            pieces += [jnp.where(row8 == 0, prev[7:8, :], zs[lo_row:lo_row + 8]),
                       zs[lo_row + 8:lo_row + ROW_TILE]]
            prev_ref[j, s] = z[lo_row + ROW_TILE - 8:lo_row + ROW_TILE, :]
        return z + mu_ref[j] * (jnp.concatenate(pieces, axis=0) - z)

    r = shifted(r_ref, 0)
    k = shifted(k_ref, 1)
    v = shifted(v_ref, 2)
    lo = shifted(lo_ref, 3)

    wlog = -_softplus(-(w0_ref[...] + _dot(jnp.tanh(lo), w2_ref[...]))) - 0.5
    ld = -jnp.exp(wlog)
    a = _sigmoid(a0_ref[...] + _dot(lo, a2_ref[...]))
    gate = _dot(_sigmoid(lo), g2_ref[...])
    kk = k * kk_ref[...]
    kk = kk / jnp.maximum(jnp.sqrt(_gsum(kk * kk, ones)), 1e-12)
    k = k * (1.0 + (a - 1.0) * ka_ref[...])
    bonus = _gsum(r * k * rk_ref[...], ones) * v

    cl = _seg_cumsum(ld, c_len)
    e_pos = jnp.exp(cl)
    e_neg = jnp.exp(-cl)
    at_s[...] = (-kk * jnp.exp(cl - ld)).astype(BF16)
    bt_s[...] = (kk * a * e_neg).astype(BF16)
    kt_s[...] = (k * e_neg).astype(BF16)
    rt_s[...] = (r * e_pos).astype(BF16)
    v_s[...] = v.astype(BF16)
    w_chunk = [e_pos[(c + 1) * c_len - 1:(c + 1) * c_len, :] for c in range(n_chunks)]
    yield

    n4 = N_HEADS * c_len
    ri = lax.broadcasted_iota(jnp.int32, (n4, n4), 0)
    ci = lax.broadcasted_iota(jnp.int32, (n4, n4), 1)
    same_head = (ri >> 6) == (ci >> 6)
    wt = lax.broadcasted_iota(jnp.int32, (c_len, n4), 0)
    ws = lax.broadcasted_iota(jnp.int32, (c_len, n4), 1) & (c_len - 1)
    strict_w = ws < wt
    incl_w = ws <= wt

    chunks = range(n_chunks)
    rows = [slice(c * c_len, (c + 1) * c_len) for c in chunks]
    gram = [_dot_nt(jnp.concatenate([at_s[sl, :], rt_s[sl, :]], axis=0),
                    jnp.concatenate([_stack_heads(bt_s[sl, :], lane_head),
                                     _stack_heads(kt_s[sl, :], lane_head)], axis=0)) for sl in rows]
    a_ab = [jnp.where(strict_w, g[0:c_len, 0:n4], 0.0) for g in gram]
    yield
    t_wide = yield from _tri_inverse(a_ab, same_head, wt, ws)
    yield
    for c, sl in enumerate(rows):
        g = gram[c]
        a_ak = jnp.where(strict_w, g[0:c_len, n4:2 * n4], 0.0)
        l_rb = jnp.where(incl_w, g[c_len:2 * c_len, 0:n4], 0.0)
        l_rk = jnp.where(incl_w, g[c_len:2 * c_len, n4:2 * n4], 0.0)
        t_w = t_wide[c]
        v_stack = _stack_heads(v_s[sl, :], lane_head)
        ta = _dot(t_w, _stack_heads(at_s[sl, :], lane_head))
        uv = _dot(t_w, _stack_heads(_dot(a_ak, v_stack), lane_head))
        rf_s[sl, :] = (rt_s[sl, :] + _dot(l_rb, _stack_heads(ta, lane_head))).astype(BF16)
        y_s[sl, :] = _dot(l_rb, _stack_heads(uv, lane_head)) + _dot(l_rk, v_stack)
        bt = bt_s[sl, :]
        g_s[c] = (jnp.where(same_head, _dot_tn(ta, bt), 0.0) * w_chunk[c]).astype(BF16)
        n_s[c] = jnp.where(same_head, _dot_tn(jnp.concatenate([uv.astype(BF16), v_s[sl, :]], axis=0),
                                              jnp.concatenate([bt, kt_s[sl, :]], axis=0)),
                           0.0) * w_chunk[c]
        if c % 2 == 1:
            yield

    for step in range(per_seq):
        for s in range(MIX_SEQS):
            c = s * per_seq + step
            st = st_ref[s]
            stb = st.astype(BF16)
            h0_s[c] = stb
            st_ref[s] = st * w_chunk[c] + jnp.dot(stb, g_s[c], preferred_element_type=F32) + n_s[c]
        yield True

    for c, sl in enumerate(rows):
        y_s[sl, :] = y_s[sl, :] + _dot_nt(rf_s[sl, :], h0_s[c])

    y = y_s[...]
    mean = _gsum(y, ones) * (1.0 / HEAD_DIM)
    d = y - mean
    var = _gsum(d * d, ones) * (1.0 / HEAD_DIM)
    y = d * lax.rsqrt(var + RW_LNX_EPS) * lng_ref[...] + lnb_ref[...]
    o_ref[...] = ((y + bonus) * gate).reshape(MIX_SEQS, ROW_TILE, GROUP)


def _gla_steps(qk_ref, v_ref, g_ref, m_ref, aup_ref, ab_ref, ng_ref, o_ref,
               st_ref, qe_s, ke_s, k2_s, qi_s, v_s, o_s):
    c_len = GLA_CHUNK
    per_seq = ROW_TILE // c_len
    n_chunks = MIX_SEQS * per_seq
    kw = N_HEADS * GLA_DK
    lg = _log_sigmoid(_dot(_rows(m_ref), aup_ref[...]) + ab_ref[...]) * (1.0 / GLA_TAU)
    b = _seg_cumsum(lg, c_len).reshape(n_chunks, c_len, kw)
    b_mid = b[:, c_len // 2:c_len // 2 + 1, :]
    b_last = b[:, c_len - 1:c_len, :]
    qk = _rows(qk_ref)
    q = (qk[:, 0:kw] * (GLA_DK ** -0.5)).reshape(n_chunks, c_len, kw)
    k = qk[:, kw:2 * kw].reshape(n_chunks, c_len, kw)
    qe_s[...] = (q * jnp.exp(b - b_mid)).reshape(MIX_ROWS, kw).astype(BF16)
    ke_s[...] = (k * jnp.exp(b_mid - b)).reshape(MIX_ROWS, kw).astype(BF16)
    k2_s[...] = (k * jnp.exp(b_last - b)).reshape(MIX_ROWS, kw).astype(BF16)
    qi_s[...] = (q * jnp.exp(b)).reshape(MIX_ROWS, kw).astype(BF16)
    v_s[...] = _rows(v_ref).astype(BF16)
    decay = [jnp.exp(b_last[c]) for c in range(n_chunks)]
    yield

    k_head = _lane_head(kw, 5)
    v_head = _lane_head(GROUP, 6)
    causal = ((lax.broadcasted_iota(jnp.int32, (c_len, N_HEADS * c_len), 1) & (c_len - 1))
              <= lax.broadcasted_iota(jnp.int32, (c_len, N_HEADS * c_len), 0))
    same_head = ((lax.broadcasted_iota(jnp.int32, (GROUP, kw), 0) >> 6)
                 == (lax.broadcasted_iota(jnp.int32, (GROUP, kw), 1) >> 5))

    rows = [slice(c * c_len, (c + 1) * c_len) for c in range(n_chunks)]
    inc = [jnp.where(same_head, _dot_tn(v_s[sl, :], k2_s[sl, :]), 0.0) for sl in rows]
    yield
    entering = []
    for s in range(MIX_SEQS):
        st = st_ref[s]
        for c in range(s * per_seq, (s + 1) * per_seq):
            entering.append(st.astype(BF16))
            st = st * decay[c] + inc[c]
        st_ref[s] = st
    yield
    for c, sl in enumerate(rows):
        att = jnp.where(causal, _dot_nt(qe_s[sl, :], _stack_heads(ke_s[sl, :], k_head)), 0.0)
        o_s[sl, :] = _dot(att, _stack_heads(v_s[sl, :], v_head)) + _dot_nt(qi_s[sl, :], entering[c])
        if c % 2 == 1:
            yield

    o = o_s[...]
    ms = _gsum(o * o, _group_ones(GROUP, 6)) * (1.0 / HEAD_DIM)
    g = _rows(g_ref)
    o_ref[...] = (o * lax.rsqrt(ms + GLA_NORM_EPS) * ng_ref[...]
                  * (g * _sigmoid(g))).reshape(MIX_SEQS, ROW_TILE, GROUP)


N_RW_CHUNKS = MIX_ROWS // RW_CHUNK
_KEY_LANES = N_HEADS * GLA_DK
RW_SCRATCH = ([pltpu.VMEM((4, MIX_SEQS, 8, GROUP), F32), pltpu.VMEM((MIX_SEQS, GROUP, GROUP), F32)]
              + [pltpu.VMEM((MIX_ROWS, GROUP), BF16)] * 6 + [pltpu.VMEM((MIX_ROWS, GROUP), F32)]
              + [pltpu.VMEM((N_RW_CHUNKS, GROUP, GROUP), BF16), pltpu.VMEM((N_RW_CHUNKS, GROUP, GROUP), F32),
                 pltpu.VMEM((N_RW_CHUNKS, GROUP, GROUP), BF16)])
GLA_SCRATCH = ([pltpu.VMEM((MIX_SEQS, GROUP, _KEY_LANES), F32)] + [pltpu.VMEM((MIX_ROWS, _KEY_LANES), BF16)] * 4
               + [pltpu.VMEM((MIX_ROWS, GROUP), BF16), pltpu.VMEM((MIX_ROWS, GROUP), F32)])


def _mixers_kernel(*refs):
    it = iter(refs)
    take = lambda n: [next(it) for _ in range(n)]
    sg_in, rw_in, gla_in = take(6), take(15), take(7)
    sg_out, rw_out, gla_out = take(1), take(1), take(1)
    rw_scr, gla_scr = take(len(RW_SCRATCH)), take(len(GLA_SCRATCH))

    @pl.when(pl.program_id(1) == 0)
    def _():
        for carried in (rw_scr[0], rw_scr[1], gla_scr[0]):
            carried[...] = jnp.zeros_like(carried)

    rwkv = _rwkv_steps(*rw_in, *rw_out, *rw_scr)
    filler = itertools.chain(_gla_steps(*gla_in, *gla_out, *gla_scr), _sg_steps(*sg_in, *sg_out))
    for latency_bound in rwkv:
        if latency_bound:
            next(filler, None)
    for _ in filler:
        pass


def _token_mixers(z, p, layer, batch, seq):
    assert batch % MIX_SEQS == 0
    z = z.reshape(batch, seq, P_PACK)
    kw = N_HEADS * GLA_DK
    row = lambda c: pl.BlockSpec((MIX_SEQS, ROW_TILE, GROUP), lambda b, i, c=c: (b, i, c))
    par = lambda *shape: pl.BlockSpec((None,) + shape, lambda b, i: (layer,) + (0,) * len(shape))
    vec, mat = par(1, GROUP), par(GROUP, GROUP)
    out = pl.BlockSpec((MIX_SEQS, ROW_TILE, GROUP), lambda b, i: (b, i, 0))
    y = jax.ShapeDtypeStruct((batch, seq, GROUP), F32)
    ys = pl.pallas_call(
        _mixers_kernel,
        grid=(batch // MIX_SEQS, seq // ROW_TILE),
        in_specs=[row(0), row(1), vec, vec, par(N_HEADS, SG_CHUNK, SG_CHUNK), par(SG_CHUNK, GROUP),
                  row(2), row(3), row(4), row(5), par(4, 1, GROUP),
                  vec, vec, mat, mat, mat, vec, vec, vec, vec, vec,
                  row(6), row(7), row(8),
                  pl.BlockSpec((MIX_SEQS, ROW_TILE, 128), lambda b, i: (b, i, MISC_BLOCK)),
                  par(128, kw), par(1, kw), vec],
        out_specs=[out, out, out],
        out_shape=[y, y, y],
        scratch_shapes=RW_SCRATCH + GLA_SCRATCH,
        compiler_params=_cparams("parallel", "arbitrary"),
        name="token_mixers",
    )(z, z, p["sg_ln_g"], p["sg_ln_b"], p["sg_w"], p["sg_bias"],
      z, z, z, z, p["rw_mu"], p["rw_w0"], p["rw_a0"], p["rw_w2"], p["rw_a2"], p["rw_g2"],
      p["rw_kk"], p["rw_ka"], p["rw_rk"], p["rw_lnx_g"], p["rw_lnx_b"],
      z, z, z, z, p["gla_a_up"], p["gla_a_b"], p["gla_norm_g"])
    return [a.reshape(batch * seq, GROUP) for a in ys]


def _fox_aug_lane(h):
    return HEAD_DIM - (h % 2) * HEAD_DIM


def _fox_placement():
    pq = np.zeros((3 * 128, N_HEADS * 128), np.float32)
    pk = np.zeros((3 * 128, N_HEADS * 128), np.float32)
    one_q = np.zeros((1, N_HEADS * 128), np.float32)
    one_k = np.zeros((1, N_HEADS * 128), np.float32)
    for h in range(N_HEADS):
        lane0 = h * 128 + _fox_aug_lane(h)
        for n in range(3):
            pq[n * 128 + MISC_FOX_LANE + h, lane0 + n] = 1.0
            pk[n * 128 + MISC_FOX_LANE + h, lane0 + 3 + n] = -1.0
        one_q[0, lane0 + 3:lane0 + 6] = 1.0
        one_k[0, lane0:lane0 + 3] = 1.0
    return (jnp.asarray(pq, BF16), jnp.asarray(pk, BF16), jnp.asarray(one_q), jnp.asarray(one_k))


def _fox_prep_kernel(q_ref, k_ref, v_ref, z_ref, fb_ref, pq_ref, pk_ref, oq_ref, ok_ref,
                     qa_ref, ka_ref, vb_ref):
    c = _seg_cumsum(_log_sigmoid(z_ref[...] + fb_ref[...]), z_ref.shape[0]) * LOG2_E
    c1 = c.astype(BF16)
    rest = c - c1.astype(F32)
    c2 = rest.astype(BF16)
    c3 = (rest - c2.astype(F32)).astype(BF16)
    split = jnp.concatenate([c1, c2, c3], axis=1)
    aug_q = jnp.dot(split, pq_ref[...], preferred_element_type=F32) + oq_ref[...]
    aug_k = jnp.dot(split, pk_ref[...], preferred_element_type=F32) + ok_ref[...]
    lane = lax.broadcasted_iota(jnp.int32, (1, 128), 1)
    for h in range(N_HEADS):
        tile = slice((h // 2) * 128, (h // 2 + 1) * 128)
        mine = slice(h * 128, (h + 1) * 128)
        in_head = (lane >> 6) == (h % 2)
        qa = jnp.where(in_head, q_ref[:, tile] * (HEAD_DIM ** -0.5 * LOG2_E), aug_q[:, mine])
        ka = jnp.where(in_head, k_ref[:, tile], aug_k[:, mine])
        qa_ref[h] = qa.astype(BF16)
        ka_ref[h] = ka.astype(BF16)
    vb_ref[...] = v_ref[...].astype(BF16)


def _fox_prep(z, fb, layer, batch, seq):
    t = z.shape[0]
    col = lambda c: pl.BlockSpec((seq, GROUP), lambda b, c=c: (b, c))
    head = pl.BlockSpec((N_HEADS, seq, 128), lambda b: (0, b, 0))
    whole = lambda a: pl.BlockSpec(a.shape, lambda b: (0, 0))
    placement = _fox_placement()
    return pl.pallas_call(
        _fox_prep_kernel,
        grid=(batch,),
        in_specs=[col(9), col(10), col(11),
                  pl.BlockSpec((seq, 128), lambda b: (b, MISC_BLOCK)),
                  pl.BlockSpec((None, 1, 128), lambda b: (layer, 0, 0))] + [whole(a) for a in placement],
        out_specs=[head, head, pl.BlockSpec((seq, GROUP), lambda b: (b, 0))],
        out_shape=[jax.ShapeDtypeStruct((N_HEADS, t, 128), BF16),
                   jax.ShapeDtypeStruct((N_HEADS, t, 128), BF16),
                   jax.ShapeDtypeStruct((t, GROUP), BF16)],
        compiler_params=_cparams("parallel"),
        name="fox_prep",
    )(z, z, z, z, fb, *placement)


def _fox_kernel(qi_ref, kj_ref, qa_ref, ka_ref, v_ref, o_ref, m_ref, l_ref, acc_ref):
    tq = ROW_TILE
    i = qi_ref[pl.program_id(1)]
    j = kj_ref[pl.program_id(1)]

    @pl.when(j == 0)
    def _():
        m_ref[...] = jnp.full_like(m_ref, -1e30)
        l_ref[...] = jnp.zeros_like(l_ref)
        acc_ref[...] = jnp.zeros_like(acc_ref)

    lane_head = _lane_head(GROUP, 6)

    def update(on_diagonal):
        vb = v_ref[...]
        acc = acc_ref[...]
        if on_diagonal:
            causal = (lax.broadcasted_iota(jnp.int32, (tq, tq), 1)
                      <= lax.broadcasted_iota(jnp.int32, (tq, tq), 0))
        heads = range(N_HEADS)
        s = [lax.dot_general(qa_ref[h], ka_ref[h], (((1,), (1,)), ((), ())),
                             preferred_element_type=F32) for h in heads]
        if on_diagonal:
            s = [jnp.where(causal, s[h], -1e30) for h in heads]
        m_prev = [m_ref[h] for h in heads]
        m_new = [jnp.maximum(m_prev[h], jnp.max(s[h], axis=-1, keepdims=True)) for h in heads]
        alpha = [jnp.exp2(m_prev[h] - m_new[h]) for h in heads]
        p = [jnp.exp2(s[h] - jnp.concatenate([m_new[h]] * (tq // 128), axis=1)) for h in heads]
        for h in heads:
            l_ref[h] = alpha[h] * l_ref[h] + jnp.sum(p[h], axis=-1, keepdims=True)
            m_ref[h] = m_new[h]
        pv = [jnp.dot(p[h].astype(BF16), vb, preferred_element_type=F32) for h in heads]
        pv = jnp.where(lane_head < 2, jnp.where(lane_head == 0, pv[0], pv[1]),
                       jnp.where(lane_head == 2, pv[2], pv[3]))
        return acc * per_head(alpha) + pv

    first_half = lax.broadcasted_iota(jnp.int32, (1, 128), 1) < HEAD_DIM

    def per_head(xs):
        return jnp.concatenate([jnp.where(first_half, xs[0], xs[1]),
                                jnp.where(first_half, xs[2], xs[3])], axis=1)

    @pl.when(j < i)
    def _():
        acc_ref[...] = update(False)

    @pl.when(j == i)
    def _():
        acc = update(True)
        o_ref[...] = acc * per_head([1.0 / l_ref[h] for h in range(N_HEADS)])


def _fox(qa, ka, vb, batch, seq):
    t = vb.shape[0]
    tiles = seq // ROW_TILE
    pairs = [(i, j) for i in range(tiles) for j in range(i + 1)]
    q_tile = jnp.asarray([i for i, _ in pairs], jnp.int32)
    k_tile = jnp.asarray([j for _, j in pairs], jnp.int32)
    return pl.pallas_call(
        _fox_kernel,
        grid_spec=pltpu.PrefetchScalarGridSpec(
            num_scalar_prefetch=2,
            grid=(batch, len(pairs)),
            in_specs=[pl.BlockSpec((N_HEADS, ROW_TILE, 128),
                                   lambda b, t, qi, kj: (0, b * tiles + qi[t], 0)),
                      pl.BlockSpec((N_HEADS, ROW_TILE, 128),
                                   lambda b, t, qi, kj: (0, b * tiles + kj[t], 0)),
                      pl.BlockSpec((ROW_TILE, GROUP), lambda b, t, qi, kj: (b * tiles + kj[t], 0))],
            out_specs=pl.BlockSpec((ROW_TILE, GROUP), lambda b, t, qi, kj: (b * tiles + qi[t], 0)),
            scratch_shapes=[pltpu.VMEM((N_HEADS, ROW_TILE, 128), F32),
                            pltpu.VMEM((N_HEADS, ROW_TILE, 128), F32),
                            pltpu.VMEM((ROW_TILE, GROUP), F32)]),
        out_shape=jax.ShapeDtypeStruct((t, GROUP), F32),
        compiler_params=_cparams("parallel", "arbitrary"),
        name="fox_attention",
    )(q_tile, k_tile, qa, ka, vb)


def _memkv_kernel(mem_ref, g_ref, b_ref, w_ref, o_ref, memn_ref):
    @pl.when(pl.program_id(0) == 0)
    def _():
        memn_ref[...] = _layernorm(mem_ref[...], g_ref[...], b_ref[...]).astype(BF16)

    o_ref[...] = jnp.dot(memn_ref[...], w_ref[...], preferred_element_type=F32).astype(BF16)


def _memkv(mem, g, b, w_kv):
    rows = mem.shape[0]
    n = w_kv.shape[0]
    return pl.pallas_call(
        _memkv_kernel,
        grid=(n,),
        in_specs=[pl.BlockSpec((rows, D_MODEL), lambda i: (0, 0)),
                  pl.BlockSpec((1, D_MODEL), lambda i: (0, 0)),
                  pl.BlockSpec((1, D_MODEL), lambda i: (0, 0)),
                  pl.BlockSpec((None, D_MODEL, D_MODEL), lambda i: (i, 0, 0))],
        out_specs=pl.BlockSpec((None, rows, D_MODEL), lambda i: (i, 0, 0)),
        out_shape=jax.ShapeDtypeStruct((n, rows, D_MODEL), BF16),
        scratch_shapes=[pltpu.VMEM((rows, D_MODEL), BF16)],
        compiler_params=_cparams("arbitrary"),
        name="memory_kv",
    )(mem, g, b, w_kv)


def _mix_cross_kernel(ya_ref, yb_ref, yc_ref, yd_ref, wm_ref, x_ref, g1_ref, b1_ref,
                      k_ref, v_ref, wq_ref, wo_ref, g_ref, b_ref, o_ref):
    y = jnp.concatenate([r[...].astype(BF16) for r in (ya_ref, yb_ref, yc_ref, yd_ref)], axis=1)
    mix = jnp.dot(y, wm_ref[...], preferred_element_type=F32)
    x = _layernorm(DN_ALPHA * x_ref[...] + mix, g1_ref[...], b1_ref[...])
    q = jnp.dot(x.astype(BF16), wq_ref[...], preferred_element_type=F32).astype(BF16)
    heads = range(CA_HEADS)
    cols = [slice(h * CA_HEAD_DIM, (h + 1) * CA_HEAD_DIM) for h in heads]
    s = [_dot_nt(q[:, cols[h]], k_ref[:, cols[h]]) * (CA_HEAD_DIM ** -0.5 * LOG2_E) for h in heads]
    p = [jnp.exp2(s[h] - jnp.max(s[h], axis=-1, keepdims=True)) for h in heads]
    p = [p[h] * (1.0 / jnp.sum(p[h], axis=-1, keepdims=True)) for h in heads]
    o = [jnp.dot(p[h].astype(BF16), v_ref[:, cols[h]], preferred_element_type=F32).astype(BF16)
         for h in heads]
    ca = jnp.dot(jnp.concatenate(o, axis=1), wo_ref[...], preferred_element_type=F32)
    o_ref[...] = _layernorm(DN_ALPHA * x + ca, g_ref[...], b_ref[...])


def _mix_cross(ys, x, kv, p, layer, batch, seq):
    t = x.shape[0]
    tiles = seq // ROW_TILE
    vec = pl.BlockSpec((None, 1, D_MODEL), lambda bb, i: (layer, 0, 0))
    mat = pl.BlockSpec((None, D_MODEL, D_MODEL), lambda bb, i: (layer, 0, 0))
    y_spec = pl.BlockSpec((ROW_TILE, GROUP), lambda bb, i: (bb * tiles + i, 0))
    x_spec = pl.BlockSpec((ROW_TILE, D_MODEL), lambda bb, i: (bb * tiles + i, 0))
    return pl.pallas_call(
        _mix_cross_kernel,
        grid=(batch, tiles),
        in_specs=[y_spec, y_spec, y_spec, y_spec, mat, x_spec, vec, vec,
                  pl.BlockSpec((None, N_MEM, D_MODEL), lambda bb, i: (2 * layer, bb, 0)),
                  pl.BlockSpec((None, N_MEM, D_MODEL), lambda bb, i: (2 * layer + 1, bb, 0)),
                  mat, mat, vec, vec],
        out_specs=x_spec,
        out_shape=jax.ShapeDtypeStruct((t, D_MODEL), F32),
        compiler_params=_cparams("parallel", "parallel"),
        name="mix_cross_attention",
    )(*ys, p["w_out"], x, p["ln1_g"], p["ln1_b"], kv, kv, p["ca_wq"], p["ca_wo"],
      p["ln2_g"], p["ln2_b"])


def _ffn_kernel(x_ref, wu_ref, bu_ref, cw_ref, cb_ref, wd_ref, g_ref, b_ref, o_ref,
                act_ref, carry_ref, *, tiles_per_seq):
    tm = ROW_TILE
    tf = FFN_COL_TILE
    first = (pl.program_id(0) % tiles_per_seq) == 0

    @pl.when(first)
    def _():
        carry_ref[...] = jnp.broadcast_to(-bu_ref[...], carry_ref.shape)

    x = x_ref[...]
    xb = x.astype(BF16)
    row8 = lax.broadcasted_iota(jnp.int32, (8, tf), 0)

    def conv(col):
        cols = slice(col, col + tf)
        h = jnp.dot(xb, wu_ref[:, cols], preferred_element_type=F32)
        prev = carry_ref[:, cols]
        carry_ref[:, cols] = h[tm - 8:tm, :]
        h1 = pltpu.roll(h, 1, 0)
        h2 = pltpu.roll(h, 2, 0)
        top1 = jnp.where(row8 == 0, prev[7:8, :], h1[0:8, :])
        top2 = jnp.where(row8 == 0, prev[6:7, :], jnp.where(row8 == 1, prev[7:8, :], h2[0:8, :]))
        h1 = jnp.concatenate([top1, h1[8:tm, :]], axis=0)
        h2 = jnp.concatenate([top2, h2[8:tm, :]], axis=0)
        cw = cw_ref[:, cols]
        bias = cb_ref[:, cols] + bu_ref[:, cols] * (cw[0:1, :] + cw[1:2, :] + cw[2:3, :])
        return cw[0:1, :] * h2 + cw[1:2, :] * h1 + cw[2:3, :] * h + bias

    for c in range(D_FF // tf):
        gate = conv(c * tf)
        val = conv(D_FF + c * tf)
        act = gate * (1.0 + lax.erf(gate * (2.0 ** -0.5))) * val
        act_ref[:, c * tf:(c + 1) * tf] = act.astype(BF16)

    ff = jnp.dot(act_ref[...], wd_ref[...], preferred_element_type=F32)
    o_ref[...] = _layernorm(DN_ALPHA * x + ff, g_ref[...], b_ref[...])


def _ffn(x, p, layer, seq):
    t = x.shape[0]
    once = pl.Buffered(1)
    vec = pl.BlockSpec((None, 1, D_MODEL), lambda i: (layer, 0, 0))
    wide = lambda rows: pl.BlockSpec((None, rows, 2 * D_FF), lambda i: (layer, 0, 0))
    return pl.pallas_call(
        functools.partial(_ffn_kernel, tiles_per_seq=seq // ROW_TILE),
        grid=(t // ROW_TILE,),
        in_specs=[pl.BlockSpec((ROW_TILE, D_MODEL), lambda i: (i, 0)),
                  pl.BlockSpec((None, D_MODEL, 2 * D_FF), lambda i: (layer, 0, 0), pipeline_mode=once),
                  wide(1), wide(3), wide(1),
                  pl.BlockSpec((None, D_FF, D_MODEL), lambda i: (layer, 0, 0), pipeline_mode=once),
                  vec, vec],
        out_specs=pl.BlockSpec((ROW_TILE, D_MODEL), lambda i: (i, 0)),
        out_shape=jax.ShapeDtypeStruct((t, D_MODEL), F32),
        scratch_shapes=[pltpu.VMEM((ROW_TILE, D_FF), BF16), pltpu.VMEM((8, 2 * D_FF), F32)],
        compiler_params=_cparams("arbitrary"),
        name="conv_ffn_ln",
    )(x, p["ffn_up"], p["ffn_up_b"], p["ffn_conv"], p["ffn_conv_b"], p["ffn_down"],
      p["ln3_g"], p["ln3_b"])


def _pack_params(w_in, w_out, sg_ln_g, sg_ln_b, sg_w, sg_b, rw_mu, rw_w0, rw_w2, rw_a0, rw_a2, rw_g2,
                 rw_kk, rw_ka, rw_rk, rw_lnx_g, rw_lnx_b, gla_a_up, gla_a_b, gla_norm_g, fox_fb,
                 ln1_g, ln1_b, ca_wq, ca_wk, ca_wv, ca_wo, ln2_g, ln2_b,
                 ffn_up, ffn_up_b, ffn_conv, ffn_conv_b, ffn_down, ln3_g, ln3_b):
    depth = w_in.shape[0]
    row = lambda a: a.reshape(depth, 1, -1)

    def pad_rows(w, start):
        out = jnp.zeros((depth, GROUP, GROUP), F32)
        return out.at[:, start:start + w.shape[1], :].set(w).astype(BF16)

    fb = jnp.zeros((depth, 1, 128), F32).at[:, 0, MISC_FOX_LANE:MISC_FOX_LANE + 4].set(fox_fb)
    aup = jnp.zeros((depth, 128, N_HEADS * GLA_DK), F32).at[:, 0:16, :].set(gla_a_up)
    return {
        "w_in": w_in.astype(BF16),
        "w_out": w_out.astype(BF16),
        "sg_ln_g": row(sg_ln_g), "sg_ln_b": row(sg_ln_b), "sg_w": sg_w,
        "sg_bias": jnp.repeat(jnp.swapaxes(sg_b, 1, 2), HEAD_DIM, axis=2),
        "rw_mu": rw_mu.reshape(depth, 4, 1, GROUP),
        "rw_w0": row(rw_w0), "rw_a0": row(rw_a0),
        "rw_w2": pad_rows(rw_w2, 0), "rw_a2": pad_rows(rw_a2, 64), "rw_g2": pad_rows(rw_g2, 128),
        "rw_kk": row(rw_kk), "rw_ka": row(rw_ka), "rw_rk": row(rw_rk),
        "rw_lnx_g": row(rw_lnx_g), "rw_lnx_b": row(rw_lnx_b),
        "gla_a_up": aup.astype(BF16), "gla_a_b": row(gla_a_b), "gla_norm_g": row(gla_norm_g),
        "fox_fb": fb,
        "ln1_g": row(ln1_g), "ln1_b": row(ln1_b),
        "ca_wq": ca_wq.astype(BF16), "ca_wo": ca_wo.astype(BF16),
        "ca_wkv": jnp.stack([ca_wk, ca_wv], axis=1).reshape(2 * depth, D_MODEL, D_MODEL).astype(BF16),
        "ln2_g": row(ln2_g), "ln2_b": row(ln2_b),
        "ffn_up": ffn_up.astype(BF16), "ffn_up_b": row(ffn_up_b), "ffn_conv": ffn_conv,
        "ffn_conv_b": row(ffn_conv_b),
        "ffn_down": (0.5 * ffn_down).astype(BF16),
        "ln3_g": row(ln3_g), "ln3_b": row(ln3_b),
    }


def _mixer_outputs(x, p, layer, batch, seq):
    z = _inproj(x, p["w_in"], layer)
    y_a, y_b, y_c = _token_mixers(z, p, layer, batch, seq)
    y_d = _fox(*_fox_prep(z, p["fox_fb"], layer, batch, seq), batch, seq)
    return y_a, y_b, y_c, y_d


def kernel(x, mem, mem_ln_g, mem_ln_b, w_in, w_out, sg_ln_g, sg_ln_b, sg_w, sg_b, rw_mu, rw_w0, rw_w2, rw_a0, rw_a2, rw_g2, rw_kk, rw_ka, rw_rk, rw_lnx_g, rw_lnx_b, gla_a_up, gla_a_b, gla_norm_g, fox_fb, ln1_g, ln1_b, ca_wq, ca_wk, ca_wv, ca_wo, ln2_g, ln2_b, ffn_up, ffn_up_b, ffn_conv, ffn_conv_b, ffn_down, ln3_g, ln3_b):
    batch, seq, _ = x.shape
    depth = w_in.shape[0]
    p = _pack_params(w_in, w_out, sg_ln_g, sg_ln_b, sg_w, sg_b, rw_mu, rw_w0, rw_w2, rw_a0, rw_a2,
                     rw_g2, rw_kk, rw_ka, rw_rk, rw_lnx_g, rw_lnx_b, gla_a_up, gla_a_b, gla_norm_g,
                     fox_fb, ln1_g, ln1_b, ca_wq, ca_wk, ca_wv, ca_wo, ln2_g, ln2_b,
                     ffn_up, ffn_up_b, ffn_conv, ffn_conv_b, ffn_down, ln3_g, ln3_b)
    kv = _memkv(mem.reshape(batch * N_MEM, D_MODEL), mem_ln_g.reshape(1, -1), mem_ln_b.reshape(1, -1),
                p["ca_wkv"])
    h = x.reshape(batch * seq, D_MODEL)
    for layer in range(depth):
        ys = _mixer_outputs(h, p, layer, batch, seq)
        h = _mix_cross(ys, h, kv, p, layer, batch, seq)
        h = _ffn(h, p, layer, seq)
    return h.reshape(batch, seq, D_MODEL)
```

```python
import functools
import itertools

import jax
import jax.numpy as jnp
import numpy as np
from jax import lax
from jax.experimental import pallas as pl
from jax.experimental.pallas import tpu as pltpu

F32 = jnp.float32
BF16 = jnp.bfloat16

D_MODEL = 1024
DEPTH = 4
N_MEM = 256
HEAD_DIM = 64
N_HEADS = 4
GROUP = 256
SG_CHUNK = 128
RW_LNX_EPS = 64e-5
RW_CHUNK = 64
GLA_DK = 32
GLA_TAU = 16.0
GLA_CHUNK = 64
GLA_NORM_EPS = 1e-5
CA_HEADS = 4
CA_HEAD_DIM = 256
D_FF = 2816
DN_ALPHA = (2.0 * DEPTH) ** 0.25
LN_EPS = 1e-5
LOG2_E = 1.4426950408889634

P_PACK = 12 * 256 + 128
W_IN_GLA = 1536
W_IN_FOX = 2320
MISC_BLOCK = 24
MISC_FOX_LANE = 16

ROW_TILE = 512
MIX_SEQS = 2
MIX_ROWS = MIX_SEQS * ROW_TILE
FFN_COL_TILE = 256
FFN_ROW_TILE = 1024
VMEM_LIMIT = 56 * 1024 * 1024


def _cparams(*sem):
    return pltpu.CompilerParams(dimension_semantics=sem, vmem_limit_bytes=VMEM_LIMIT)


def _dot(a, b):
    return jnp.dot(a.astype(BF16), b.astype(BF16), preferred_element_type=F32)


def _dot_nt(a, b):
    return lax.dot_general(a.astype(BF16), b.astype(BF16), (((1,), (1,)), ((), ())),
                           preferred_element_type=F32)


def _dot_tn(a, b):
    return lax.dot_general(a.astype(BF16), b.astype(BF16), (((0,), (0,)), ((), ())),
                           preferred_element_type=F32)


def _split2(a):
    hi = a.astype(BF16)
    lo = (a - hi.astype(F32)).astype(BF16)
    return hi, lo


def _group_ones(n, shift):
    r = lax.broadcasted_iota(jnp.int32, (n, n), 0) >> shift
    c = lax.broadcasted_iota(jnp.int32, (n, n), 1) >> shift
    return jnp.where(r == c, 1.0, 0.0).astype(BF16)


def _gsum(x, ones):
    hi, lo = _split2(x)
    return jnp.dot(hi, ones, preferred_element_type=F32) + jnp.dot(lo, ones, preferred_element_type=F32)


def _gsum_squares(x, ones):
    return jnp.dot((x * x).astype(BF16), ones, preferred_element_type=F32)


def _lane_head(width, shift):
    return lax.broadcasted_iota(jnp.int32, (1, width), 1) >> shift


def _softplus(x):
    return jnp.maximum(x, 0.0) + jnp.log1p(jnp.exp(-jnp.abs(x)))


def _log_sigmoid(x):
    return jnp.minimum(x, 0.0) - jnp.log1p(jnp.exp(-jnp.abs(x)))


def _sigmoid(x):
    return 1.0 / (1.0 + jnp.exp(-x))


def _layernorm(x, g, b):
    mu = jnp.mean(x, axis=-1, keepdims=True)
    d = x - mu
    var = jnp.mean(d * d, axis=-1, keepdims=True)
    return d * lax.rsqrt(var + LN_EPS) * g + b


def _seg_cumsum(x, seg):
    pos = lax.broadcasted_iota(jnp.int32, x.shape, 0) & (seg - 1)
    k = 1
    while k < seg:
        x = x + jnp.where(pos >= k, pltpu.roll(x, k, 0), 0.0)
        k *= 2
    return x


def _stack_heads(x, lane_head):
    return jnp.concatenate([jnp.where(lane_head == h, x, 0.0) for h in range(N_HEADS)], axis=0)


def _inproj_kernel(x_ref, w_ref, o_ref):
    z = jnp.dot(x_ref[...].astype(BF16), w_ref[...], preferred_element_type=F32)
    main = W_IN_GLA + 768
    o_ref[:, 0:main] = z[:, 0:main]
    o_ref[:, main:main + 768] = z[:, W_IN_FOX:W_IN_FOX + 768]
    tail = jnp.concatenate([z[:, main:main + 16], z[:, W_IN_FOX + 768:W_IN_FOX + 772],
                            jnp.zeros((ROW_TILE, 128 - 20), F32)], axis=1)
    o_ref[:, main + 768:P_PACK] = tail


def _inproj(x, w_in, layer):
    t = x.shape[0]
    p_in = w_in.shape[-1]
    return pl.pallas_call(
        _inproj_kernel,
        grid=(t // ROW_TILE,),
        in_specs=[pl.BlockSpec((ROW_TILE, D_MODEL), lambda i: (i, 0)),
                  pl.BlockSpec((None, D_MODEL, p_in), lambda i: (layer, 0, 0))],
        out_specs=pl.BlockSpec((ROW_TILE, P_PACK), lambda i: (i, 0)),
        out_shape=jax.ShapeDtypeStruct((t, P_PACK), F32),
        compiler_params=_cparams("parallel"),
        name="inproj",
    )(x, w_in)


def _rows(ref):
    return ref[...].reshape(MIX_ROWS, ref.shape[-1])


def _sg_steps(u_ref, v_ref, g_ref, b_ref, w_ref, sb_ref, o_ref):
    ones = _group_ones(GROUP, 6)
    lane_head = _lane_head(GROUP, 6)
    v = _rows(v_ref)
    mean = _gsum(v, ones) * (1.0 / HEAD_DIM)
    d = v - mean
    var = _gsum_squares(d, ones) * (1.0 / HEAD_DIM)
    vn = d * lax.rsqrt(var + LN_EPS) * g_ref[...] + b_ref[...]
    row = lax.broadcasted_iota(jnp.int32, (SG_CHUNK, SG_CHUNK), 0)
    col = lax.broadcasted_iota(jnp.int32, (SG_CHUNK, SG_CHUNK), 1)
    wm = [jnp.where(col <= row, w_ref[h], 0.0).astype(BF16) for h in range(N_HEADS)]
    yield
    per_seq = ROW_TILE // SG_CHUNK
    for c in range(MIX_ROWS // SG_CHUNK):
        vc = vn[c * SG_CHUNK:(c + 1) * SG_CHUNK]
        s = sb_ref[...]
        for h in range(N_HEADS):
            s = s + jnp.dot(wm[h], jnp.where(lane_head == h, vc, 0.0).astype(BF16),
                            preferred_element_type=F32)
        sl = slice((c % per_seq) * SG_CHUNK, (c % per_seq + 1) * SG_CHUNK)
        o_ref[c // per_seq, sl, :] = u_ref[c // per_seq, sl, :] * s
        yield


def _tri_inverse(mats, same_head, wt, ws):
    expand = lambda w: jnp.where(same_head, jnp.concatenate([w] * N_HEADS, axis=0), 0.0).astype(BF16)
    b16 = (wt >> 4) == (ws >> 4)
    b32 = (wt >> 5) == (ws >> 5)
    in32 = b32 & jnp.logical_not(b16)
    in64 = jnp.logical_not(b32)
    x = [jnp.where(b16, a, 0.0) for a in mats]
    t = [jnp.where(wt == ws, 1.0, 0.0) + d for d in x]
    for _ in range(3):
        x = [_dot(d, expand(d)) for d in x]
        yield True
        t = [ti + _dot(ti, expand(d)) for ti, d in zip(t, x)]
    for level in (in32, in64):
        yield True
        te = [_dot(ti, expand(jnp.where(level, a, 0.0))) for ti, a in zip(t, mats)]
        yield True
        t = [ti + _dot(tei, expand(ti)) for ti, tei in zip(t, te)]
    return t


def _rwkv_steps(r_ref, k_ref, v_ref, lo_ref, mu_ref, w0_ref, a0_ref, w2_ref, a2_ref, g2_ref,
                kk_ref, ka_ref, rk_ref, lng_ref, lnb_ref, o_ref,
                prev_ref, st_ref, at_s, bt_s, kt_s, rt_s, v_s, rf_s, y_s, g_s, n_s, h0_s):
    c_len = RW_CHUNK
    per_seq = ROW_TILE // c_len
    n_chunks = MIX_SEQS * per_seq
    ones = _group_ones(GROUP, 6)
    lane_head = _lane_head(GROUP, 6)
    row8 = lax.broadcasted_iota(jnp.int32, (8, GROUP), 0)

    def shifted(ref, j):
        z = _rows(ref)
        zs = pltpu.roll(z, 1, 0)
        pieces = []
        for s in range(MIX_SEQS):
            lo_row = s * ROW_TILE
            prev = prev_ref[j, s]
            pieces += [jnp.where(row8 == 0, prev[7:8, :], zs[lo_row:lo_row + 8]),
                       zs[lo_row + 8:lo_row + ROW_TILE]]
            prev_ref[j, s] = z[lo_row + ROW_TILE - 8:lo_row + ROW_TILE, :]
        return z + mu_ref[j] * (jnp.concatenate(pieces, axis=0) - z)

    r = shifted(r_ref, 0)
    k = shifted(k_ref, 1)
    v = shifted(v_ref, 2)
    lo = shifted(lo_ref, 3)

    wlog = -_softplus(-(w0_ref[...] + _dot(jnp.tanh(lo), w2_ref[...]))) - 0.5
    ld = -jnp.exp(wlog)
    a = _sigmoid(a0_ref[...] + _dot(lo, a2_ref[...]))
    gate = _dot(_sigmoid(lo), g2_ref[...])
    kk = k * kk_ref[...]
    kk = kk / jnp.maximum(jnp.sqrt(_gsum_squares(kk, ones)), 1e-12)
    k = k * (1.0 + (a - 1.0) * ka_ref[...])
    bonus = _gsum(r * k * rk_ref[...], ones) * v

    cl = _seg_cumsum(ld, c_len)
    e_pos = jnp.exp(cl)
    e_neg = jnp.exp(-cl)
    at_s[...] = (-kk * jnp.exp(cl - ld)).astype(BF16)
    bt_s[...] = (kk * a * e_neg).astype(BF16)
    kt_s[...] = (k * e_neg).astype(BF16)
    rt_s[...] = (r * e_pos).astype(BF16)
    v_s[...] = v.astype(BF16)
    w_chunk = [e_pos[(c + 1) * c_len - 1:(c + 1) * c_len, :] for c in range(n_chunks)]
    yield

    n4 = N_HEADS * c_len
    ri = lax.broadcasted_iota(jnp.int32, (n4, n4), 0)
    ci = lax.broadcasted_iota(jnp.int32, (n4, n4), 1)
    same_head = (ri >> 6) == (ci >> 6)
    wt = lax.broadcasted_iota(jnp.int32, (c_len, n4), 0)
    ws = lax.broadcasted_iota(jnp.int32, (c_len, n4), 1) & (c_len - 1)
    strict_w = ws < wt
    incl_w = ws <= wt

    chunks = range(n_chunks)
    rows = [slice(c * c_len, (c + 1) * c_len) for c in chunks]
    gram = [_dot_nt(jnp.concatenate([at_s[sl, :], rt_s[sl, :]], axis=0),
                    jnp.concatenate([_stack_heads(bt_s[sl, :], lane_head),
                                     _stack_heads(kt_s[sl, :], lane_head)], axis=0)) for sl in rows]
    a_ab = [jnp.where(strict_w, g[0:c_len, 0:n4], 0.0) for g in gram]
    yield
    t_wide = yield from _tri_inverse(a_ab, same_head, wt, ws)
    yield
    for c, sl in enumerate(rows):
        g = gram[c]
        a_ak = jnp.where(strict_w, g[0:c_len, n4:2 * n4], 0.0)
        l_rb = jnp.where(incl_w, g[c_len:2 * c_len, 0:n4], 0.0)
        l_rk = jnp.where(incl_w, g[c_len:2 * c_len, n4:2 * n4], 0.0)
        t_w = t_wide[c]
        v_stack = _stack_heads(v_s[sl, :], lane_head)
        ta = _dot(t_w, _stack_heads(at_s[sl, :], lane_head))
        uv = _dot(t_w, _stack_heads(_dot(a_ak, v_stack), lane_head))
        rf_s[sl, :] = (rt_s[sl, :] + _dot(l_rb, _stack_heads(ta, lane_head))).astype(BF16)
        y_s[sl, :] = _dot(l_rb, _stack_heads(uv, lane_head)) + _dot(l_rk, v_stack)
        bt = bt_s[sl, :]
        g_s[c] = (jnp.where(same_head, _dot_tn(ta, bt), 0.0) * w_chunk[c]).astype(BF16)
        n_s[c] = jnp.where(same_head, _dot_tn(jnp.concatenate([uv.astype(BF16), v_s[sl, :]], axis=0),
                                              jnp.concatenate([bt, kt_s[sl, :]], axis=0)),
                           0.0) * w_chunk[c]
        if c % 2 == 1:
            yield

    for step in range(per_seq):
        for s in range(MIX_SEQS):
            c = s * per_seq + step
            st = st_ref[s]
            stb = st.astype(BF16)
            h0_s[c] = stb
            st_ref[s] = st * w_chunk[c] + jnp.dot(stb, g_s[c], preferred_element_type=F32) + n_s[c]
        yield True

    for c, sl in enumerate(rows):
        y_s[sl, :] = y_s[sl, :] + _dot_nt(rf_s[sl, :], h0_s[c])

    y = y_s[...]
    mean = _gsum(y, ones) * (1.0 / HEAD_DIM)
    d = y - mean
    var = _gsum_squares(d, ones) * (1.0 / HEAD_DIM)
    y = d * lax.rsqrt(var + RW_LNX_EPS) * lng_ref[...] + lnb_ref[...]
    o_ref[...] = ((y + bonus) * gate).reshape(MIX_SEQS, ROW_TILE, GROUP)


def _gla_steps(qk_ref, v_ref, g_ref, m_ref, aup_ref, ab_ref, ng_ref, o_ref,
               st_ref, qe_s, ke_s, k2_s, qi_s, v_s, o_s):
    c_len = GLA_CHUNK
    per_seq = ROW_TILE // c_len
    n_chunks = MIX_SEQS * per_seq
    kw = N_HEADS * GLA_DK
    lg = _log_sigmoid(_dot(_rows(m_ref), aup_ref[...]) + ab_ref[...]) * (1.0 / GLA_TAU)
    b = _seg_cumsum(lg, c_len).reshape(n_chunks, c_len, kw)
    b_mid = b[:, c_len // 2:c_len // 2 + 1, :]
    b_last = b[:, c_len - 1:c_len, :]
    qk = _rows(qk_ref)
    q = (qk[:, 0:kw] * (GLA_DK ** -0.5)).reshape(n_chunks, c_len, kw)
    k = qk[:, kw:2 * kw].reshape(n_chunks, c_len, kw)
    qe_s[...] = (q * jnp.exp(b - b_mid)).reshape(MIX_ROWS, kw).astype(BF16)
    ke_s[...] = (k * jnp.exp(b_mid - b)).reshape(MIX_ROWS, kw).astype(BF16)
    k2_s[...] = (k * jnp.exp(b_last - b)).reshape(MIX_ROWS, kw).astype(BF16)
    qi_s[...] = (q * jnp.exp(b)).reshape(MIX_ROWS, kw).astype(BF16)
    v_s[...] = _rows(v_ref).astype(BF16)
    decay = [jnp.exp(b_last[c]) for c in range(n_chunks)]
    yield

    k_head = _lane_head(kw, 5)
    v_head = _lane_head(GROUP, 6)
    causal = ((lax.broadcasted_iota(jnp.int32, (c_len, N_HEADS * c_len), 1) & (c_len - 1))
              <= lax.broadcasted_iota(jnp.int32, (c_len, N_HEADS * c_len), 0))
    same_head = ((lax.broadcasted_iota(jnp.int32, (GROUP, kw), 0) >> 6)
                 == (lax.broadcasted_iota(jnp.int32, (GROUP, kw), 1) >> 5))

    rows = [slice(c * c_len, (c + 1) * c_len) for c in range(n_chunks)]
    inc = [jnp.where(same_head, _dot_tn(v_s[sl, :], k2_s[sl, :]), 0.0) for sl in rows]
    yield
    entering = []
    for s in range(MIX_SEQS):
        st = st_ref[s]
        for c in range(s * per_seq, (s + 1) * per_seq):
            entering.append(st.astype(BF16))
            st = st * decay[c] + inc[c]
        st_ref[s] = st
    yield
    for c, sl in enumerate(rows):
        att = jnp.where(causal, _dot_nt(qe_s[sl, :], _stack_heads(ke_s[sl, :], k_head)), 0.0)
        o_s[sl, :] = _dot(att, _stack_heads(v_s[sl, :], v_head)) + _dot_nt(qi_s[sl, :], entering[c])
        if c % 2 == 1:
            yield

    o = o_s[...]
    ms = _gsum_squares(o, _group_ones(GROUP, 6)) * (1.0 / HEAD_DIM)
    g = _rows(g_ref)
    o_ref[...] = (o * lax.rsqrt(ms + GLA_NORM_EPS) * ng_ref[...]
                  * (g * _sigmoid(g))).reshape(MIX_SEQS, ROW_TILE, GROUP)


N_RW_CHUNKS = MIX_ROWS // RW_CHUNK
_KEY_LANES = N_HEADS * GLA_DK
RW_SCRATCH = ([pltpu.VMEM((4, MIX_SEQS, 8, GROUP), F32), pltpu.VMEM((MIX_SEQS, GROUP, GROUP), F32)]
              + [pltpu.VMEM((MIX_ROWS, GROUP), BF16)] * 6 + [pltpu.VMEM((MIX_ROWS, GROUP), F32)]
              + [pltpu.VMEM((N_RW_CHUNKS, GROUP, GROUP), BF16), pltpu.VMEM((N_RW_CHUNKS, GROUP, GROUP), F32),
                 pltpu.VMEM((N_RW_CHUNKS, GROUP, GROUP), BF16)])
GLA_SCRATCH = ([pltpu.VMEM((MIX_SEQS, GROUP, _KEY_LANES), F32)] + [pltpu.VMEM((MIX_ROWS, _KEY_LANES), BF16)] * 4
               + [pltpu.VMEM((MIX_ROWS, GROUP), BF16), pltpu.VMEM((MIX_ROWS, GROUP), F32)])


def _mixers_kernel(*refs):
    it = iter(refs)
    take = lambda n: [next(it) for _ in range(n)]
    sg_in, rw_in, gla_in = take(6), take(15), take(7)
    sg_out, rw_out, gla_out = take(1), take(1), take(1)
    rw_scr, gla_scr = take(len(RW_SCRATCH)), take(len(GLA_SCRATCH))

    @pl.when(pl.program_id(1) == 0)
    def _():
        for carried in (rw_scr[0], rw_scr[1], gla_scr[0]):
            carried[...] = jnp.zeros_like(carried)

    rwkv = _rwkv_steps(*rw_in, *rw_out, *rw_scr)
    filler = itertools.chain(_gla_steps(*gla_in, *gla_out, *gla_scr), _sg_steps(*sg_in, *sg_out))
    for latency_bound in rwkv:
        if latency_bound:
            next(filler, None)
    for _ in filler:
        pass


def _token_mixers(z, p, layer, batch, seq):
    assert batch % MIX_SEQS == 0
    z = z.reshape(batch, seq, P_PACK)
    kw = N_HEADS * GLA_DK
    row = lambda c: pl.BlockSpec((MIX_SEQS, ROW_TILE, GROUP), lambda b, i, c=c: (b, i, c))
    par = lambda *shape: pl.BlockSpec((None,) + shape, lambda b, i: (layer,) + (0,) * len(shape))
    vec, mat = par(1, GROUP), par(GROUP, GROUP)
    out = pl.BlockSpec((MIX_SEQS, ROW_TILE, GROUP), lambda b, i: (b, i, 0))
    y = jax.ShapeDtypeStruct((batch, seq, GROUP), F32)
    ys = pl.pallas_call(
        _mixers_kernel,
        grid=(batch // MIX_SEQS, seq // ROW_TILE),
        in_specs=[row(0), row(1), vec, vec, par(N_HEADS, SG_CHUNK, SG_CHUNK), par(SG_CHUNK, GROUP),
                  row(2), row(3), row(4), row(5), par(4, 1, GROUP),
                  vec, vec, mat, mat, mat, vec, vec, vec, vec, vec,
                  row(6), row(7), row(8),
                  pl.BlockSpec((MIX_SEQS, ROW_TILE, 128), lambda b, i: (b, i, MISC_BLOCK)),
                  par(128, kw), par(1, kw), vec],
        out_specs=[out, out, out],
        out_shape=[y, y, y],
        scratch_shapes=RW_SCRATCH + GLA_SCRATCH,
        compiler_params=_cparams("parallel", "arbitrary"),
        name="token_mixers",
    )(z, z, p["sg_ln_g"], p["sg_ln_b"], p["sg_w"], p["sg_bias"],
      z, z, z, z, p["rw_mu"], p["rw_w0"], p["rw_a0"], p["rw_w2"], p["rw_a2"], p["rw_g2"],
      p["rw_kk"], p["rw_ka"], p["rw_rk"], p["rw_lnx_g"], p["rw_lnx_b"],
      z, z, z, z, p["gla_a_up"], p["gla_a_b"], p["gla_norm_g"])
    return [a.reshape(batch * seq, GROUP) for a in ys]


def _fox_aug_lane(h):
    return HEAD_DIM - (h % 2) * HEAD_DIM


def _fox_placement():
    pq = np.zeros((3 * 128, N_HEADS * 128), np.float32)
    pk = np.zeros((3 * 128, N_HEADS * 128), np.float32)
    one_q = np.zeros((1, N_HEADS * 128), np.float32)
    one_k = np.zeros((1, N_HEADS * 128), np.float32)
    for h in range(N_HEADS):
        lane0 = h * 128 + _fox_aug_lane(h)
        for n in range(3):
            pq[n * 128 + MISC_FOX_LANE + h, lane0 + n] = 1.0
            pk[n * 128 + MISC_FOX_LANE + h, lane0 + 3 + n] = -1.0
        one_q[0, lane0 + 3:lane0 + 6] = 1.0
        one_k[0, lane0:lane0 + 3] = 1.0
    return (jnp.asarray(pq, BF16), jnp.asarray(pk, BF16), jnp.asarray(one_q), jnp.asarray(one_k))


def _fox_prep_kernel(q_ref, k_ref, v_ref, z_ref, fb_ref, pq_ref, pk_ref, oq_ref, ok_ref,
                     qa_ref, ka_ref, vb_ref):
    c = _seg_cumsum(_log_sigmoid(z_ref[...] + fb_ref[...]), z_ref.shape[0]) * LOG2_E
    c1 = c.astype(BF16)
    rest = c - c1.astype(F32)
    c2 = rest.astype(BF16)
    c3 = (rest - c2.astype(F32)).astype(BF16)
    split = jnp.concatenate([c1, c2, c3], axis=1)
    aug_q = jnp.dot(split, pq_ref[...], preferred_element_type=F32) + oq_ref[...]
    aug_k = jnp.dot(split, pk_ref[...], preferred_element_type=F32) + ok_ref[...]
    lane = lax.broadcasted_iota(jnp.int32, (1, 128), 1)
    for h in range(N_HEADS):
        tile = slice((h // 2) * 128, (h // 2 + 1) * 128)
        mine = slice(h * 128, (h + 1) * 128)
        in_head = (lane >> 6) == (h % 2)
        qa = jnp.where(in_head, q_ref[:, tile] * (HEAD_DIM ** -0.5 * LOG2_E), aug_q[:, mine])
        ka = jnp.where(in_head, k_ref[:, tile], aug_k[:, mine])
        qa_ref[h] = qa.astype(BF16)
        ka_ref[h] = ka.astype(BF16)
    vb_ref[...] = v_ref[...].astype(BF16)


def _fox_prep(z, fb, layer, batch, seq):
    t = z.shape[0]
    col = lambda c: pl.BlockSpec((seq, GROUP), lambda b, c=c: (b, c))
    head = pl.BlockSpec((N_HEADS, seq, 128), lambda b: (0, b, 0))
    whole = lambda a: pl.BlockSpec(a.shape, lambda b: (0, 0))
    placement = _fox_placement()
    return pl.pallas_call(
        _fox_prep_kernel,
        grid=(batch,),
        in_specs=[col(9), col(10), col(11),
                  pl.BlockSpec((seq, 128), lambda b: (b, MISC_BLOCK)),
                  pl.BlockSpec((None, 1, 128), lambda b: (layer, 0, 0))] + [whole(a) for a in placement],
        out_specs=[head, head, pl.BlockSpec((seq, GROUP), lambda b: (b, 0))],
        out_shape=[jax.ShapeDtypeStruct((N_HEADS, t, 128), BF16),
                   jax.ShapeDtypeStruct((N_HEADS, t, 128), BF16),
                   jax.ShapeDtypeStruct((t, GROUP), BF16)],
        compiler_params=_cparams("parallel"),
        name="fox_prep",
    )(z, z, z, z, fb, *placement)


def _fox_kernel(qi_ref, kj_ref, qa_ref, ka_ref, v_ref, o_ref, m_ref, l_ref, acc_ref):
    tq = ROW_TILE
    i = qi_ref[pl.program_id(1)]
    j = kj_ref[pl.program_id(1)]

    @pl.when(j == 0)
    def _():
        m_ref[...] = jnp.full_like(m_ref, -1e30)
        l_ref[...] = jnp.zeros_like(l_ref)
        acc_ref[...] = jnp.zeros_like(acc_ref)

    lane_head = _lane_head(GROUP, 6)

    def update(on_diagonal):
        vb = v_ref[...]
        acc = acc_ref[...]
        if on_diagonal:
            causal = (lax.broadcasted_iota(jnp.int32, (tq, tq), 1)
                      <= lax.broadcasted_iota(jnp.int32, (tq, tq), 0))
        heads = range(N_HEADS)
        s = [lax.dot_general(qa_ref[h], ka_ref[h], (((1,), (1,)), ((), ())),
                             preferred_element_type=F32) for h in heads]
        if on_diagonal:
            s = [jnp.where(causal, s[h], -1e30) for h in heads]
        m_prev = [m_ref[h] for h in heads]
        m_new = [jnp.maximum(m_prev[h], jnp.max(s[h], axis=-1, keepdims=True)) for h in heads]
        alpha = [jnp.exp2(m_prev[h] - m_new[h]) for h in heads]
        p = [jnp.exp2(s[h] - jnp.concatenate([m_new[h]] * (tq // 128), axis=1)) for h in heads]
        for h in heads:
            l_ref[h] = alpha[h] * l_ref[h] + jnp.sum(p[h], axis=-1, keepdims=True)
            m_ref[h] = m_new[h]
        pv = [jnp.dot(p[h].astype(BF16), vb, preferred_element_type=F32) for h in heads]
        pv = jnp.where(lane_head < 2, jnp.where(lane_head == 0, pv[0], pv[1]),
                       jnp.where(lane_head == 2, pv[2], pv[3]))
        return acc * per_head(alpha) + pv

    first_half = lax.broadcasted_iota(jnp.int32, (1, 128), 1) < HEAD_DIM

    def per_head(xs):
        return jnp.concatenate([jnp.where(first_half, xs[0], xs[1]),
                                jnp.where(first_half, xs[2], xs[3])], axis=1)

    @pl.when(j < i)
    def _():
        acc_ref[...] = update(False)

    @pl.when(j == i)
    def _():
        acc = update(True)
        o_ref[...] = acc * per_head([1.0 / l_ref[h] for h in range(N_HEADS)])


def _fox(qa, ka, vb, batch, seq):
    t = vb.shape[0]
    tiles = seq // ROW_TILE
    pairs = [(i, j) for i in range(tiles) for j in range(i + 1)]
    q_tile = jnp.asarray([i for i, _ in pairs], jnp.int32)
    k_tile = jnp.asarray([j for _, j in pairs], jnp.int32)
    return pl.pallas_call(
        _fox_kernel,
        grid_spec=pltpu.PrefetchScalarGridSpec(
            num_scalar_prefetch=2,
            grid=(batch, len(pairs)),
            in_specs=[pl.BlockSpec((N_HEADS, ROW_TILE, 128),
                                   lambda b, t, qi, kj: (0, b * tiles + qi[t], 0)),
                      pl.BlockSpec((N_HEADS, ROW_TILE, 128),
                                   lambda b, t, qi, kj: (0, b * tiles + kj[t], 0)),
                      pl.BlockSpec((ROW_TILE, GROUP), lambda b, t, qi, kj: (b * tiles + kj[t], 0))],
            out_specs=pl.BlockSpec((ROW_TILE, GROUP), lambda b, t, qi, kj: (b * tiles + qi[t], 0)),
            scratch_shapes=[pltpu.VMEM((N_HEADS, ROW_TILE, 128), F32),
                            pltpu.VMEM((N_HEADS, ROW_TILE, 128), F32),
                            pltpu.VMEM((ROW_TILE, GROUP), F32)]),
        out_shape=jax.ShapeDtypeStruct((t, GROUP), F32),
        compiler_params=_cparams("parallel", "arbitrary"),
        name="fox_attention",
    )(q_tile, k_tile, qa, ka, vb)


def _memkv_kernel(mem_ref, g_ref, b_ref, w_ref, o_ref, memn_ref):
    @pl.when(pl.program_id(0) == 0)
    def _():
        memn_ref[...] = _layernorm(mem_ref[...], g_ref[...], b_ref[...]).astype(BF16)

    o_ref[...] = jnp.dot(memn_ref[...], w_ref[...], preferred_element_type=F32).astype(BF16)


def _memkv(mem, g, b, w_kv):
    rows = mem.shape[0]
    n = w_kv.shape[0]
    return pl.pallas_call(
        _memkv_kernel,
        grid=(n,),
        in_specs=[pl.BlockSpec((rows, D_MODEL), lambda i: (0, 0)),
                  pl.BlockSpec((1, D_MODEL), lambda i: (0, 0)),
                  pl.BlockSpec((1, D_MODEL), lambda i: (0, 0)),
                  pl.BlockSpec((None, D_MODEL, D_MODEL), lambda i: (i, 0, 0))],
        out_specs=pl.BlockSpec((None, rows, D_MODEL), lambda i: (i, 0, 0)),
        out_shape=jax.ShapeDtypeStruct((n, rows, D_MODEL), BF16),
        scratch_shapes=[pltpu.VMEM((rows, D_MODEL), BF16)],
        compiler_params=_cparams("arbitrary"),
        name="memory_kv",
    )(mem, g, b, w_kv)


def _mix_cross_kernel(ya_ref, yb_ref, yc_ref, yd_ref, wm_ref, x_ref, g1_ref, b1_ref,
                      k_ref, v_ref, wq_ref, wo_ref, g_ref, b_ref, o_ref):
    y = jnp.concatenate([r[...].astype(BF16) for r in (ya_ref, yb_ref, yc_ref, yd_ref)], axis=1)
    mix = jnp.dot(y, wm_ref[...], preferred_element_type=F32)
    x = _layernorm(DN_ALPHA * x_ref[...] + mix, g1_ref[...], b1_ref[...])
    q = jnp.dot(x.astype(BF16), wq_ref[...], preferred_element_type=F32).astype(BF16)
    heads = range(CA_HEADS)
    cols = [slice(h * CA_HEAD_DIM, (h + 1) * CA_HEAD_DIM) for h in heads]
    s = [_dot_nt(q[:, cols[h]], k_ref[:, cols[h]]) * (CA_HEAD_DIM ** -0.5 * LOG2_E) for h in heads]
    p = [jnp.exp2(s[h] - jnp.max(s[h], axis=-1, keepdims=True)) for h in heads]
    p = [p[h] * (1.0 / jnp.sum(p[h], axis=-1, keepdims=True)) for h in heads]
    o = [jnp.dot(p[h].astype(BF16), v_ref[:, cols[h]], preferred_element_type=F32).astype(BF16)
         for h in heads]
    ca = jnp.dot(jnp.concatenate(o, axis=1), wo_ref[...], preferred_element_type=F32)
    o_ref[...] = _layernorm(DN_ALPHA * x + ca, g_ref[...], b_ref[...])


def _mix_cross(ys, x, kv, p, layer, batch, seq):
    t = x.shape[0]
    tiles = seq // ROW_TILE
    vec = pl.BlockSpec((None, 1, D_MODEL), lambda bb, i: (layer, 0, 0))
    mat = pl.BlockSpec((None, D_MODEL, D_MODEL), lambda bb, i: (layer, 0, 0))
    y_spec = pl.BlockSpec((ROW_TILE, GROUP), lambda bb, i: (bb * tiles + i, 0))
    x_spec = pl.BlockSpec((ROW_TILE, D_MODEL), lambda bb, i: (bb * tiles + i, 0))
    return pl.pallas_call(
        _mix_cross_kernel,
        grid=(batch, tiles),
        in_specs=[y_spec, y_spec, y_spec, y_spec, mat, x_spec, vec, vec,
                  pl.BlockSpec((None, N_MEM, D_MODEL), lambda bb, i: (2 * layer, bb, 0)),
                  pl.BlockSpec((None, N_MEM, D_MODEL), lambda bb, i: (2 * layer + 1, bb, 0)),
                  mat, mat, vec, vec],
        out_specs=x_spec,
        out_shape=jax.ShapeDtypeStruct((t, D_MODEL), F32),
        compiler_params=_cparams("parallel", "parallel"),
        name="mix_cross_attention",
    )(*ys, p["w_out"], x, p["ln1_g"], p["ln1_b"], kv, kv, p["ca_wq"], p["ca_wo"],
      p["ln2_g"], p["ln2_b"])


def _ffn_kernel(x_ref, wu_ref, bu_ref, cw_ref, cb_ref, wd_ref, g_ref, b_ref, o_ref,
                act_ref, carry_ref, *, tiles_per_seq):
    tm = FFN_ROW_TILE
    tf = FFN_COL_TILE
    first = (pl.program_id(0) % tiles_per_seq) == 0

    @pl.when(first)
    def _():
        carry_ref[...] = jnp.broadcast_to(-bu_ref[...], carry_ref.shape)

    x = x_ref[...]
    xb = x.astype(BF16)
    row8 = lax.broadcasted_iota(jnp.int32, (8, tf), 0)

    def conv(col, scale):
        cols = slice(col, col + tf)
        h = jnp.dot(xb, wu_ref[:, cols], preferred_element_type=F32)
        prev = carry_ref[:, cols]
        carry_ref[:, cols] = h[tm - 8:tm, :]
        h1 = pltpu.roll(h, 1, 0)
        h2 = pltpu.roll(h, 2, 0)
        top1 = jnp.where(row8 == 0, prev[7:8, :], h1[0:8, :])
        top2 = jnp.where(row8 == 0, prev[6:7, :], jnp.where(row8 == 1, prev[7:8, :], h2[0:8, :]))
        h1 = jnp.concatenate([top1, h1[8:tm, :]], axis=0)
        h2 = jnp.concatenate([top2, h2[8:tm, :]], axis=0)
        cw = cw_ref[:, cols]
        bias = (cb_ref[:, cols] + bu_ref[:, cols] * (cw[0:1, :] + cw[1:2, :] + cw[2:3, :])) * scale
        cw = cw * scale
        return cw[0:1, :] * h2 + cw[1:2, :] * h1 + cw[2:3, :] * h + bias

    for c in range(D_FF // tf):
        gate = conv(c * tf, 2.0 ** -0.5)
        val = conv(D_FF + c * tf, 2.0 ** 0.5)
        act = gate * (1.0 + lax.erf(gate)) * val
        act_ref[:, c * tf:(c + 1) * tf] = act.astype(BF16)

    ff = jnp.dot(act_ref[...], wd_ref[...], preferred_element_type=F32)
    o_ref[...] = _layernorm(DN_ALPHA * x + ff, g_ref[...], b_ref[...])


def _ffn(x, p, layer, seq):
    t = x.shape[0]
    once = pl.Buffered(1)
    vec = pl.BlockSpec((None, 1, D_MODEL), lambda i: (layer, 0, 0))
    wide = lambda rows: pl.BlockSpec((None, rows, 2 * D_FF), lambda i: (layer, 0, 0))
    return pl.pallas_call(
        functools.partial(_ffn_kernel, tiles_per_seq=seq // FFN_ROW_TILE),
        grid=(t // FFN_ROW_TILE,),
        in_specs=[pl.BlockSpec((FFN_ROW_TILE, D_MODEL), lambda i: (i, 0)),
                  pl.BlockSpec((None, D_MODEL, 2 * D_FF), lambda i: (layer, 0, 0), pipeline_mode=once),
                  wide(1), wide(3), wide(1),
                  pl.BlockSpec((None, D_FF, D_MODEL), lambda i: (layer, 0, 0), pipeline_mode=once),
                  vec, vec],
        out_specs=pl.BlockSpec((FFN_ROW_TILE, D_MODEL), lambda i: (i, 0)),
        out_shape=jax.ShapeDtypeStruct((t, D_MODEL), F32),
        scratch_shapes=[pltpu.VMEM((FFN_ROW_TILE, D_FF), BF16), pltpu.VMEM((8, 2 * D_FF), F32)],
        compiler_params=_cparams("arbitrary"),
        name="conv_ffn_ln",
    )(x, p["ffn_up"], p["ffn_up_b"], p["ffn_conv"], p["ffn_conv_b"], p["ffn_down"],
      p["ln3_g"], p["ln3_b"])


def _pack_params(w_in, w_out, sg_ln_g, sg_ln_b, sg_w, sg_b, rw_mu, rw_w0, rw_w2, rw_a0, rw_a2, rw_g2,
                 rw_kk, rw_ka, rw_rk, rw_lnx_g, rw_lnx_b, gla_a_up, gla_a_b, gla_norm_g, fox_fb,
                 ln1_g, ln1_b, ca_wq, ca_wk, ca_wv, ca_wo, ln2_g, ln2_b,
                 ffn_up, ffn_up_b, ffn_conv, ffn_conv_b, ffn_down, ln3_g, ln3_b):
    depth = w_in.shape[0]
    row = lambda a: a.reshape(depth, 1, -1)

    def pad_rows(w, start):
        out = jnp.zeros((depth, GROUP, GROUP), F32)
        return out.at[:, start:start + w.shape[1], :].set(w).astype(BF16)

    fb = jnp.zeros((depth, 1, 128), F32).at[:, 0, MISC_FOX_LANE:MISC_FOX_LANE + 4].set(fox_fb)
    aup = jnp.zeros((depth, 128, N_HEADS * GLA_DK), F32).at[:, 0:16, :].set(gla_a_up)
    return {
        "w_in": w_in.astype(BF16),
        "w_out": w_out.astype(BF16),
        "sg_ln_g": row(sg_ln_g), "sg_ln_b": row(sg_ln_b), "sg_w": sg_w,
        "sg_bias": jnp.repeat(jnp.swapaxes(sg_b, 1, 2), HEAD_DIM, axis=2),
        "rw_mu": rw_mu.reshape(depth, 4, 1, GROUP),
        "rw_w0": row(rw_w0), "rw_a0": row(rw_a0),
        "rw_w2": pad_rows(rw_w2, 0), "rw_a2": pad_rows(rw_a2, 64), "rw_g2": pad_rows(rw_g2, 128),
        "rw_kk": row(rw_kk), "rw_ka": row(rw_ka), "rw_rk": row(rw_rk),
        "rw_lnx_g": row(rw_lnx_g), "rw_lnx_b": row(rw_lnx_b),
        "gla_a_up": aup.astype(BF16), "gla_a_b": row(gla_a_b), "gla_norm_g": row(gla_norm_g),
        "fox_fb": fb,
        "ln1_g": row(ln1_g), "ln1_b": row(ln1_b),
        "ca_wq": ca_wq.astype(BF16), "ca_wo": ca_wo.astype(BF16),
        "ca_wkv": jnp.stack([ca_wk, ca_wv], axis=1).reshape(2 * depth, D_MODEL, D_MODEL).astype(BF16),
        "ln2_g": row(ln2_g), "ln2_b": row(ln2_b),
        "ffn_up": ffn_up.astype(BF16), "ffn_up_b": row(ffn_up_b), "ffn_conv": ffn_conv,
        "ffn_conv_b": row(ffn_conv_b),
        "ffn_down": (0.5 * ffn_down).astype(BF16),
        "ln3_g": row(ln3_g), "ln3_b": row(ln3_b),
    }


def _mixer_outputs(x, p, layer, batch, seq):
    z = _inproj(x, p["w_in"], layer)
    y_a, y_b, y_c = _token_mixers(z, p, layer, batch, seq)
    y_d = _fox(*_fox_prep(z, p["fox_fb"], layer, batch, seq), batch, seq)
    return y_a, y_b, y_c, y_d


def kernel(x, mem, mem_ln_g, mem_ln_b, w_in, w_out, sg_ln_g, sg_ln_b, sg_w, sg_b, rw_mu, rw_w0, rw_w2, rw_a0, rw_a2, rw_g2, rw_kk, rw_ka, rw_rk, rw_lnx_g, rw_lnx_b, gla_a_up, gla_a_b, gla_norm_g, fox_fb, ln1_g, ln1_b, ca_wq, ca_wk, ca_wv, ca_wo, ln2_g, ln2_b, ffn_up, ffn_up_b, ffn_conv, ffn_conv_b, ffn_down, ln3_g, ln3_b):
    batch, seq, _ = x.shape
    depth = w_in.shape[0]
    p = _pack_params(w_in, w_out, sg_ln_g, sg_ln_b, sg_w, sg_b, rw_mu, rw_w0, rw_w2, rw_a0, rw_a2,
                     rw_g2, rw_kk, rw_ka, rw_rk, rw_lnx_g, rw_lnx_b, gla_a_up, gla_a_b, gla_norm_g,
                     fox_fb, ln1_g, ln1_b, ca_wq, ca_wk, ca_wv, ca_wo, ln2_g, ln2_b,
                     ffn_up, ffn_up_b, ffn_conv, ffn_conv_b, ffn_down, ln3_g, ln3_b)
    kv = _memkv(mem.reshape(batch * N_MEM, D_MODEL), mem_ln_g.reshape(1, -1), mem_ln_b.reshape(1, -1),
                p["ca_wkv"])
    h = x.reshape(batch * seq, D_MODEL)
    for layer in range(depth):
        ys = _mixer_outputs(h, p, layer, batch, seq)
        h = _mix_cross(ys, h, kv, p, layer, batch, seq)
        h = _ffn(h, p, layer, seq)
    return h.reshape(batch, seq, D_MODEL)
```

```python
import functools
import itertools

import jax
import jax.numpy as jnp
import numpy as np
from jax import lax
from jax.experimental import pallas as pl
from jax.experimental.pallas import tpu as pltpu

F32 = jnp.float32
BF16 = jnp.bfloat16

D_MODEL = 1024
DEPTH = 4
N_MEM = 256
HEAD_DIM = 64
N_HEADS = 4
GROUP = 256
SG_CHUNK = 128
RW_LNX_EPS = 64e-5
RW_CHUNK = 64
GLA_DK = 32
GLA_TAU = 16.0
GLA_CHUNK = 64
GLA_NORM_EPS = 1e-5
CA_HEADS = 4
CA_HEAD_DIM = 256
D_FF = 2816
DN_ALPHA = (2.0 * DEPTH) ** 0.25
LN_EPS = 1e-5
LOG2_E = 1.4426950408889634

P_PACK = 12 * 256 + 128
W_IN_GLA = 1536
W_IN_FOX = 2320
MISC_BLOCK = 24
MISC_FOX_LANE = 16

ROW_TILE = 512
MIX_SEQS = 2
MIX_ROWS = MIX_SEQS * ROW_TILE
FOX_SEQS = 2
FFN_COL_TILE = 256
FFN_ROW_TILE = 1024
VMEM_LIMIT = 56 * 1024 * 1024


def _cparams(*sem):
    return pltpu.CompilerParams(dimension_semantics=sem, vmem_limit_bytes=VMEM_LIMIT)


def _dot(a, b):
    return jnp.dot(a.astype(BF16), b.astype(BF16), preferred_element_type=F32)


def _dot_nt(a, b):
    return lax.dot_general(a.astype(BF16), b.astype(BF16), (((1,), (1,)), ((), ())),
                           preferred_element_type=F32)


def _dot_tn(a, b):
    return lax.dot_general(a.astype(BF16), b.astype(BF16), (((0,), (0,)), ((), ())),
                           preferred_element_type=F32)


def _split2(a):
    hi = a.astype(BF16)
    lo = (a - hi.astype(F32)).astype(BF16)
    return hi, lo


def _group_ones(n, shift):
    r = lax.broadcasted_iota(jnp.int32, (n, n), 0) >> shift
    c = lax.broadcasted_iota(jnp.int32, (n, n), 1) >> shift
    return jnp.where(r == c, 1.0, 0.0).astype(BF16)


def _gsum(x, ones):
    hi, lo = _split2(x)
    return jnp.dot(hi, ones, preferred_element_type=F32) + jnp.dot(lo, ones, preferred_element_type=F32)


def _gsum_squares(x, ones):
    return jnp.dot((x * x).astype(BF16), ones, preferred_element_type=F32)


def _lane_head(width, shift):
    return lax.broadcasted_iota(jnp.int32, (1, width), 1) >> shift


def _softplus(x):
    return jnp.maximum(x, 0.0) + jnp.log1p(jnp.exp(-jnp.abs(x)))


def _log_sigmoid(x):
    return jnp.minimum(x, 0.0) - jnp.log1p(jnp.exp(-jnp.abs(x)))


def _sigmoid(x):
    return 1.0 / (1.0 + jnp.exp(-x))


def _layernorm(x, g, b):
    mu = jnp.mean(x, axis=-1, keepdims=True)
    d = x - mu
    var = jnp.mean(d * d, axis=-1, keepdims=True)
    return d * lax.rsqrt(var + LN_EPS) * g + b


def _seg_cumsum(x, seg):
    pos = lax.broadcasted_iota(jnp.int32, x.shape, 0) & (seg - 1)
    k = 1
    while k < seg:
        x = x + jnp.where(pos >= k, pltpu.roll(x, k, 0), 0.0)
        k *= 2
    return x


def _stack_heads(x, lane_head):
    return jnp.concatenate([jnp.where(lane_head == h, x, 0.0) for h in range(N_HEADS)], axis=0)


def _inproj_kernel(x_ref, w_ref, o_ref):
    z = jnp.dot(x_ref[...].astype(BF16), w_ref[...], preferred_element_type=F32)
    main = W_IN_GLA + 768
    o_ref[:, 0:main] = z[:, 0:main]
    o_ref[:, main:main + 768] = z[:, W_IN_FOX:W_IN_FOX + 768]
    tail = jnp.concatenate([z[:, main:main + 16], z[:, W_IN_FOX + 768:W_IN_FOX + 772],
                            jnp.zeros((ROW_TILE, 128 - 20), F32)], axis=1)
    o_ref[:, main + 768:P_PACK] = tail


def _inproj(x, w_in, layer):
    t = x.shape[0]
    p_in = w_in.shape[-1]
    return pl.pallas_call(
        _inproj_kernel,
        grid=(t // ROW_TILE,),
        in_specs=[pl.BlockSpec((ROW_TILE, D_MODEL), lambda i: (i, 0)),
                  pl.BlockSpec((None, D_MODEL, p_in), lambda i: (layer, 0, 0))],
        out_specs=pl.BlockSpec((ROW_TILE, P_PACK), lambda i: (i, 0)),
        out_shape=jax.ShapeDtypeStruct((t, P_PACK), F32),
        compiler_params=_cparams("parallel"),
        name="inproj",
    )(x, w_in)


def _rows(ref):
    return ref[...].reshape(MIX_ROWS, ref.shape[-1])


def _sg_steps(u_ref, v_ref, g_ref, b_ref, w_ref, sb_ref, o_ref):
    ones = _group_ones(GROUP, 6)
    lane_head = _lane_head(GROUP, 6)
    v = _rows(v_ref)
    mean = _gsum(v, ones) * (1.0 / HEAD_DIM)
    d = v - mean
    var = _gsum_squares(d, ones) * (1.0 / HEAD_DIM)
    vn = d * lax.rsqrt(var + LN_EPS) * g_ref[...] + b_ref[...]
    row = lax.broadcasted_iota(jnp.int32, (SG_CHUNK, SG_CHUNK), 0)
    col = lax.broadcasted_iota(jnp.int32, (SG_CHUNK, SG_CHUNK), 1)
    wm = [jnp.where(col <= row, w_ref[h], 0.0).astype(BF16) for h in range(N_HEADS)]
    yield
    per_seq = ROW_TILE // SG_CHUNK
    for c in range(MIX_ROWS // SG_CHUNK):
        vc = vn[c * SG_CHUNK:(c + 1) * SG_CHUNK]
        s = sb_ref[...]
        for h in range(N_HEADS):
            s = s + jnp.dot(wm[h], jnp.where(lane_head == h, vc, 0.0).astype(BF16),
                            preferred_element_type=F32)
        sl = slice((c % per_seq) * SG_CHUNK, (c % per_seq + 1) * SG_CHUNK)
        o_ref[c // per_seq, sl, :] = u_ref[c // per_seq, sl, :] * s
        yield


def _tri_inverse(mats, same_head, wt, ws):
    expand = lambda w: jnp.where(same_head, jnp.concatenate([w] * N_HEADS, axis=0), 0.0).astype(BF16)
    b16 = (wt >> 4) == (ws >> 4)
    b32 = (wt >> 5) == (ws >> 5)
    in32 = b32 & jnp.logical_not(b16)
    in64 = jnp.logical_not(b32)
    x = [jnp.where(b16, a, 0.0) for a in mats]
    t = [jnp.where(wt == ws, 1.0, 0.0) + d for d in x]
    for _ in range(3):
        x = [_dot(d, expand(d)) for d in x]
        yield True
        t = [ti + _dot(ti, expand(d)) for ti, d in zip(t, x)]
    for level in (in32, in64):
        yield True
        te = [_dot(ti, expand(jnp.where(level, a, 0.0))) for ti, a in zip(t, mats)]
        yield True
        t = [ti + _dot(tei, expand(ti)) for ti, tei in zip(t, te)]
    return t


def _rwkv_steps(r_ref, k_ref, v_ref, lo_ref, mu_ref, w0_ref, a0_ref, w2_ref, a2_ref, g2_ref,
                kk_ref, ka_ref, rk_ref, lng_ref, lnb_ref, o_ref,
                prev_ref, st_ref, at_s, bt_s, kt_s, rt_s, v_s, rf_s, y_s, g_s, n_s, h0_s):
    c_len = RW_CHUNK
    per_seq = ROW_TILE // c_len
    n_chunks = MIX_SEQS * per_seq
    ones = _group_ones(GROUP, 6)
    lane_head = _lane_head(GROUP, 6)
    row8 = lax.broadcasted_iota(jnp.int32, (8, GROUP), 0)

    def shifted(ref, j):
        z = _rows(ref)
        zs = pltpu.roll(z, 1, 0)
        pieces = []
        for s in range(MIX_SEQS):
            lo_row = s * ROW_TILE
            prev = prev_ref[j, s]
            pieces += [jnp.where(row8 == 0, prev[7:8, :], zs[lo_row:lo_row + 8]),
                       zs[lo_row + 8:lo_row + ROW_TILE]]
            prev_ref[j, s] = z[lo_row + ROW_TILE - 8:lo_row + ROW_TILE, :]
        return z + mu_ref[j] * (jnp.concatenate(pieces, axis=0) - z)

    r = shifted(r_ref, 0)
    k = shifted(k_ref, 1)
    v = shifted(v_ref, 2)
    lo = shifted(lo_ref, 3)

    wlog = -_softplus(-(w0_ref[...] + _dot(jnp.tanh(lo), w2_ref[...]))) - 0.5
    ld = -jnp.exp(wlog)
    a = _sigmoid(a0_ref[...] + _dot(lo, a2_ref[...]))
    gate = _dot(_sigmoid(lo), g2_ref[...])
    kk = k * kk_ref[...]
    kk = kk / jnp.maximum(jnp.sqrt(_gsum_squares(kk, ones)), 1e-12)
    k = k * (1.0 + (a - 1.0) * ka_ref[...])
    bonus = _gsum(r * k * rk_ref[...], ones) * v

    cl = _seg_cumsum(ld, c_len)
    e_pos = jnp.exp(cl)
    e_neg = jnp.exp(-cl)
    at_s[...] = (-kk * jnp.exp(cl - ld)).astype(BF16)
    bt_s[...] = (kk * a * e_neg).astype(BF16)
    kt_s[...] = (k * e_neg).astype(BF16)
    rt_s[...] = (r * e_pos).astype(BF16)
    v_s[...] = v.astype(BF16)
    w_chunk = [e_pos[(c + 1) * c_len - 1:(c + 1) * c_len, :] for c in range(n_chunks)]
    yield

    n4 = N_HEADS * c_len
    ri = lax.broadcasted_iota(jnp.int32, (n4, n4), 0)
    ci = lax.broadcasted_iota(jnp.int32, (n4, n4), 1)
    same_head = (ri >> 6) == (ci >> 6)
    wt = lax.broadcasted_iota(jnp.int32, (c_len, n4), 0)
    ws = lax.broadcasted_iota(jnp.int32, (c_len, n4), 1) & (c_len - 1)
    strict_w = ws < wt
    incl_w = ws <= wt

    chunks = range(n_chunks)
    rows = [slice(c * c_len, (c + 1) * c_len) for c in chunks]
    gram = [_dot_nt(jnp.concatenate([at_s[sl, :], rt_s[sl, :]], axis=0),
                    jnp.concatenate([_stack_heads(bt_s[sl, :], lane_head),
                                     _stack_heads(kt_s[sl, :], lane_head)], axis=0)) for sl in rows]
    a_ab = [jnp.where(strict_w, g[0:c_len, 0:n4], 0.0) for g in gram]
    yield
    t_wide = yield from _tri_inverse(a_ab, same_head, wt, ws)
    yield
    for c, sl in enumerate(rows):
        g = gram[c]
        a_ak = jnp.where(strict_w, g[0:c_len, n4:2 * n4], 0.0)
        l_rb = jnp.where(incl_w, g[c_len:2 * c_len, 0:n4], 0.0)
        l_rk = jnp.where(incl_w, g[c_len:2 * c_len, n4:2 * n4], 0.0)
        t_w = t_wide[c]
        v_stack = _stack_heads(v_s[sl, :], lane_head)
        ta = _dot(t_w, _stack_heads(at_s[sl, :], lane_head))
        uv = _dot(t_w, _stack_heads(_dot(a_ak, v_stack), lane_head))
        rf_s[sl, :] = (rt_s[sl, :] + _dot(l_rb, _stack_heads(ta, lane_head))).astype(BF16)
        y_s[sl, :] = _dot(l_rb, _stack_heads(uv, lane_head)) + _dot(l_rk, v_stack)
        bt = bt_s[sl, :]
        g_s[c] = (jnp.where(same_head, _dot_tn(ta, bt), 0.0) * w_chunk[c]).astype(BF16)
        n_s[c] = jnp.where(same_head, _dot_tn(jnp.concatenate([uv.astype(BF16), v_s[sl, :]], axis=0),
                                              jnp.concatenate([bt, kt_s[sl, :]], axis=0)),
                           0.0) * w_chunk[c]
        if c % 2 == 1:
            yield

    for step in range(per_seq):
        for s in range(MIX_SEQS):
            c = s * per_seq + step
            st = st_ref[s]
            stb = st.astype(BF16)
            h0_s[c] = stb
            st_ref[s] = st * w_chunk[c] + jnp.dot(stb, g_s[c], preferred_element_type=F32) + n_s[c]
        yield True

    for c, sl in enumerate(rows):
        y_s[sl, :] = y_s[sl, :] + _dot_nt(rf_s[sl, :], h0_s[c])

    y = y_s[...]
    mean = _gsum(y, ones) * (1.0 / HEAD_DIM)
    d = y - mean
    var = _gsum_squares(d, ones) * (1.0 / HEAD_DIM)
    y = d * lax.rsqrt(var + RW_LNX_EPS) * lng_ref[...] + lnb_ref[...]
    o_ref[...] = ((y + bonus) * gate).reshape(MIX_SEQS, ROW_TILE, GROUP)


def _gla_steps(qk_ref, v_ref, g_ref, m_ref, aup_ref, ab_ref, ng_ref, o_ref,
               st_ref, qe_s, ke_s, k2_s, qi_s, v_s, o_s):
    c_len = GLA_CHUNK
    per_seq = ROW_TILE // c_len
    n_chunks = MIX_SEQS * per_seq
    kw = N_HEADS * GLA_DK
    lg = _log_sigmoid(_dot(_rows(m_ref), aup_ref[...]) + ab_ref[...]) * (1.0 / GLA_TAU)
    b = _seg_cumsum(lg, c_len).reshape(n_chunks, c_len, kw)
    b_mid = b[:, c_len // 2:c_len // 2 + 1, :]
    b_last = b[:, c_len - 1:c_len, :]
    qk = _rows(qk_ref)
    q = (qk[:, 0:kw] * (GLA_DK ** -0.5)).reshape(n_chunks, c_len, kw)
    k = qk[:, kw:2 * kw].reshape(n_chunks, c_len, kw)
    qe_s[...] = (q * jnp.exp(b - b_mid)).reshape(MIX_ROWS, kw).astype(BF16)
    ke_s[...] = (k * jnp.exp(b_mid - b)).reshape(MIX_ROWS, kw).astype(BF16)
    k2_s[...] = (k * jnp.exp(b_last - b)).reshape(MIX_ROWS, kw).astype(BF16)
    qi_s[...] = (q * jnp.exp(b)).reshape(MIX_ROWS, kw).astype(BF16)
    v_s[...] = _rows(v_ref).astype(BF16)
    decay = [jnp.exp(b_last[c]) for c in range(n_chunks)]
    yield

    k_head = _lane_head(kw, 5)
    v_head = _lane_head(GROUP, 6)
    causal = ((lax.broadcasted_iota(jnp.int32, (c_len, N_HEADS * c_len), 1) & (c_len - 1))
              <= lax.broadcasted_iota(jnp.int32, (c_len, N_HEADS * c_len), 0))
    same_head = ((lax.broadcasted_iota(jnp.int32, (GROUP, kw), 0) >> 6)
                 == (lax.broadcasted_iota(jnp.int32, (GROUP, kw), 1) >> 5))

    rows = [slice(c * c_len, (c + 1) * c_len) for c in range(n_chunks)]
    inc = [jnp.where(same_head, _dot_tn(v_s[sl, :], k2_s[sl, :]), 0.0) for sl in rows]
    yield
    entering = []
    for s in range(MIX_SEQS):
        st = st_ref[s]
        for c in range(s * per_seq, (s + 1) * per_seq):
            entering.append(st.astype(BF16))
            st = st * decay[c] + inc[c]
        st_ref[s] = st
    yield
    for c, sl in enumerate(rows):
        att = jnp.where(causal, _dot_nt(qe_s[sl, :], _stack_heads(ke_s[sl, :], k_head)), 0.0)
        o_s[sl, :] = _dot(att, _stack_heads(v_s[sl, :], v_head)) + _dot_nt(qi_s[sl, :], entering[c])
        if c % 2 == 1:
            yield

    o = o_s[...]
    ms = _gsum_squares(o, _group_ones(GROUP, 6)) * (1.0 / HEAD_DIM)
    g = _rows(g_ref)
    o_ref[...] = (o * lax.rsqrt(ms + GLA_NORM_EPS) * ng_ref[...]
                  * (g * _sigmoid(g))).reshape(MIX_SEQS, ROW_TILE, GROUP)


N_RW_CHUNKS = MIX_ROWS // RW_CHUNK
_KEY_LANES = N_HEADS * GLA_DK
RW_SCRATCH = ([pltpu.VMEM((4, MIX_SEQS, 8, GROUP), F32), pltpu.VMEM((MIX_SEQS, GROUP, GROUP), F32)]
              + [pltpu.VMEM((MIX_ROWS, GROUP), BF16)] * 6 + [pltpu.VMEM((MIX_ROWS, GROUP), F32)]
              + [pltpu.VMEM((N_RW_CHUNKS, GROUP, GROUP), BF16), pltpu.VMEM((N_RW_CHUNKS, GROUP, GROUP), F32),
                 pltpu.VMEM((N_RW_CHUNKS, GROUP, GROUP), BF16)])
GLA_SCRATCH = ([pltpu.VMEM((MIX_SEQS, GROUP, _KEY_LANES), F32)] + [pltpu.VMEM((MIX_ROWS, _KEY_LANES), BF16)] * 4
               + [pltpu.VMEM((MIX_ROWS, GROUP), BF16), pltpu.VMEM((MIX_ROWS, GROUP), F32)])


def _mixers_kernel(*refs):
    it = iter(refs)
    take = lambda n: [next(it) for _ in range(n)]
    sg_in, rw_in, gla_in = take(6), take(15), take(7)
    sg_out, rw_out, gla_out = take(1), take(1), take(1)
    rw_scr, gla_scr = take(len(RW_SCRATCH)), take(len(GLA_SCRATCH))

    @pl.when(pl.program_id(1) == 0)
    def _():
        for carried in (rw_scr[0], rw_scr[1], gla_scr[0]):
            carried[...] = jnp.zeros_like(carried)

    rwkv = _rwkv_steps(*rw_in, *rw_out, *rw_scr)
    filler = itertools.chain(_gla_steps(*gla_in, *gla_out, *gla_scr), _sg_steps(*sg_in, *sg_out))
    for latency_bound in rwkv:
        if latency_bound:
            next(filler, None)
    for _ in filler:
        pass


def _token_mixers(z, p, layer, batch, seq):
    assert batch % MIX_SEQS == 0
    z = z.reshape(batch, seq, P_PACK)
    kw = N_HEADS * GLA_DK
    row = lambda c: pl.BlockSpec((MIX_SEQS, ROW_TILE, GROUP), lambda b, i, c=c: (b, i, c))
    par = lambda *shape: pl.BlockSpec((None,) + shape, lambda b, i: (layer,) + (0,) * len(shape))
    vec, mat = par(1, GROUP), par(GROUP, GROUP)
    out = pl.BlockSpec((MIX_SEQS, ROW_TILE, GROUP), lambda b, i: (b, i, 0))
    y = jax.ShapeDtypeStruct((batch, seq, GROUP), F32)
    ys = pl.pallas_call(
        _mixers_kernel,
        grid=(batch // MIX_SEQS, seq // ROW_TILE),
        in_specs=[row(0), row(1), vec, vec, par(N_HEADS, SG_CHUNK, SG_CHUNK), par(SG_CHUNK, GROUP),
                  row(2), row(3), row(4), row(5), par(4, 1, GROUP),
                  vec, vec, mat, mat, mat, vec, vec, vec, vec, vec,
                  row(6), row(7), row(8),
                  pl.BlockSpec((MIX_SEQS, ROW_TILE, 128), lambda b, i: (b, i, MISC_BLOCK)),
                  par(128, kw), par(1, kw), vec],
        out_specs=[out, out, out],
        out_shape=[y, y, y],
        scratch_shapes=RW_SCRATCH + GLA_SCRATCH,
        compiler_params=_cparams("parallel", "arbitrary"),
        name="token_mixers",
    )(z, z, p["sg_ln_g"], p["sg_ln_b"], p["sg_w"], p["sg_bias"],
      z, z, z, z, p["rw_mu"], p["rw_w0"], p["rw_a0"], p["rw_w2"], p["rw_a2"], p["rw_g2"],
      p["rw_kk"], p["rw_ka"], p["rw_rk"], p["rw_lnx_g"], p["rw_lnx_b"],
      z, z, z, z, p["gla_a_up"], p["gla_a_b"], p["gla_norm_g"])
    return [a.reshape(batch * seq, GROUP) for a in ys]


def _fox_aug_lane(h):
    return HEAD_DIM - (h % 2) * HEAD_DIM


def _fox_placement():
    pq = np.zeros((3 * 128, N_HEADS * 128), np.float32)
    pk = np.zeros((3 * 128, N_HEADS * 128), np.float32)
    one_q = np.zeros((1, N_HEADS * 128), np.float32)
    one_k = np.zeros((1, N_HEADS * 128), np.float32)
    for h in range(N_HEADS):
        lane0 = h * 128 + _fox_aug_lane(h)
        for n in range(3):
            pq[n * 128 + MISC_FOX_LANE + h, lane0 + n] = 1.0
            pk[n * 128 + MISC_FOX_LANE + h, lane0 + 3 + n] = -1.0
        one_q[0, lane0 + 3:lane0 + 6] = 1.0
        one_k[0, lane0:lane0 + 3] = 1.0
    return (jnp.asarray(pq, BF16), jnp.asarray(pk, BF16), jnp.asarray(one_q), jnp.asarray(one_k))


def _fox_prep_kernel(q_ref, k_ref, v_ref, z_ref, fb_ref, pq_ref, pk_ref, oq_ref, ok_ref,
                     qa_ref, ka_ref, vb_ref):
    c = _seg_cumsum(_log_sigmoid(z_ref[...] + fb_ref[...]), z_ref.shape[0]) * LOG2_E
    c1 = c.astype(BF16)
    rest = c - c1.astype(F32)
    c2 = rest.astype(BF16)
    c3 = (rest - c2.astype(F32)).astype(BF16)
    split = jnp.concatenate([c1, c2, c3], axis=1)
    aug_q = jnp.dot(split, pq_ref[...], preferred_element_type=F32) + oq_ref[...]
    aug_k = jnp.dot(split, pk_ref[...], preferred_element_type=F32) + ok_ref[...]
    lane = lax.broadcasted_iota(jnp.int32, (1, 128), 1)
    for h in range(N_HEADS):
        tile = slice((h // 2) * 128, (h // 2 + 1) * 128)
        mine = slice(h * 128, (h + 1) * 128)
        in_head = (lane >> 6) == (h % 2)
        qa = jnp.where(in_head, q_ref[:, tile] * (HEAD_DIM ** -0.5 * LOG2_E), aug_q[:, mine])
        ka = jnp.where(in_head, k_ref[:, tile], aug_k[:, mine])
        qa_ref[h] = qa.astype(BF16)
        ka_ref[h] = ka.astype(BF16)
    vb_ref[...] = v_ref[...].astype(BF16)


def _fox_prep(z, fb, layer, batch, seq):
    t = z.shape[0]
    col = lambda c: pl.BlockSpec((seq, GROUP), lambda b, c=c: (b, c))
    head = pl.BlockSpec((N_HEADS, seq, 128), lambda b: (0, b, 0))
    whole = lambda a: pl.BlockSpec(a.shape, lambda b: (0, 0))
    placement = _fox_placement()
    return pl.pallas_call(
        _fox_prep_kernel,
        grid=(batch,),
        in_specs=[col(9), col(10), col(11),
                  pl.BlockSpec((seq, 128), lambda b: (b, MISC_BLOCK)),
                  pl.BlockSpec((None, 1, 128), lambda b: (layer, 0, 0))] + [whole(a) for a in placement],
        out_specs=[head, head, pl.BlockSpec((seq, GROUP), lambda b: (b, 0))],
        out_shape=[jax.ShapeDtypeStruct((N_HEADS, t, 128), BF16),
                   jax.ShapeDtypeStruct((N_HEADS, t, 128), BF16),
                   jax.ShapeDtypeStruct((t, GROUP), BF16)],
        compiler_params=_cparams("parallel"),
        name="fox_prep",
    )(z, z, z, z, fb, *placement)


def _fox_kernel(qi_ref, kj_ref, qa_ref, ka_ref, v_ref, o_ref, m_ref, l_ref, acc_ref):
    tq = ROW_TILE
    i = qi_ref[pl.program_id(1)]
    j = kj_ref[pl.program_id(1)]

    @pl.when(j == 0)
    def _():
        m_ref[...] = jnp.full_like(m_ref, -1e30)
        l_ref[...] = jnp.zeros_like(l_ref)
        acc_ref[...] = jnp.zeros_like(acc_ref)

    lane_head = _lane_head(GROUP, 6)
    units = [(b, h) for b in range(FOX_SEQS) for h in range(N_HEADS)]

    def update(on_diagonal):
        if on_diagonal:
            causal = (lax.broadcasted_iota(jnp.int32, (tq, tq), 1)
                      <= lax.broadcasted_iota(jnp.int32, (tq, tq), 0))
        s = [lax.dot_general(qa_ref[h, b], ka_ref[h, b], (((1,), (1,)), ((), ())),
                             preferred_element_type=F32) for b, h in units]
        if on_diagonal:
            s = [jnp.where(causal, x, -1e30) for x in s]
        m_prev = [m_ref[u] for u in range(len(units))]
        m_new = [jnp.maximum(mp, jnp.max(x, axis=-1, keepdims=True)) for mp, x in zip(m_prev, s)]
        alpha = [jnp.exp2(mp - mn) for mp, mn in zip(m_prev, m_new)]
        p = [jnp.exp2(x - jnp.concatenate([mn] * (tq // 128), axis=1)) for x, mn in zip(s, m_new)]
        for u in range(len(units)):
            l_ref[u] = alpha[u] * l_ref[u] + jnp.sum(p[u], axis=-1, keepdims=True)
            m_ref[u] = m_new[u]
        pv = [jnp.dot(p[u].astype(BF16), v_ref[b], preferred_element_type=F32)
              for u, (b, h) in enumerate(units)]
        out = []
        for b in range(FOX_SEQS):
            mine = slice(b * N_HEADS, (b + 1) * N_HEADS)
            pvb = pv[mine]
            pvb = jnp.where(lane_head < 2, jnp.where(lane_head == 0, pvb[0], pvb[1]),
                            jnp.where(lane_head == 2, pvb[2], pvb[3]))
            out.append(acc_ref[b] * per_head(alpha[mine]) + pvb)
        return out

    first_half = lax.broadcasted_iota(jnp.int32, (1, 128), 1) < HEAD_DIM

    def per_head(xs):
        return jnp.concatenate([jnp.where(first_half, xs[0], xs[1]),
                                jnp.where(first_half, xs[2], xs[3])], axis=1)

    @pl.when(j < i)
    def _():
        for b, acc in enumerate(update(False)):
            acc_ref[b] = acc

    @pl.when(j == i)
    def _():
        for b, acc in enumerate(update(True)):
            o_ref[b] = acc * per_head([1.0 / l_ref[b * N_HEADS + h] for h in range(N_HEADS)])


def _fox(qa, ka, vb, batch, seq):
    assert batch % FOX_SEQS == 0
    tiles = seq // ROW_TILE
    pairs = [(i, j) for i in range(tiles) for j in range(i + 1)]
    q_tile = jnp.asarray([i for i, _ in pairs], jnp.int32)
    k_tile = jnp.asarray([j for _, j in pairs], jnp.int32)
    qa = qa.reshape(N_HEADS, batch, seq, 128)
    ka = ka.reshape(N_HEADS, batch, seq, 128)
    vb = vb.reshape(batch, seq, GROUP)
    head_q = pl.BlockSpec((N_HEADS, FOX_SEQS, ROW_TILE, 128), lambda b, t, qi, kj: (0, b, qi[t], 0))
    head_k = pl.BlockSpec((N_HEADS, FOX_SEQS, ROW_TILE, 128), lambda b, t, qi, kj: (0, b, kj[t], 0))
    out = pl.pallas_call(
        _fox_kernel,
        grid_spec=pltpu.PrefetchScalarGridSpec(
            num_scalar_prefetch=2,
            grid=(batch // FOX_SEQS, len(pairs)),
            in_specs=[head_q, head_k,
                      pl.BlockSpec((FOX_SEQS, ROW_TILE, GROUP), lambda b, t, qi, kj: (b, kj[t], 0))],
            out_specs=pl.BlockSpec((FOX_SEQS, ROW_TILE, GROUP), lambda b, t, qi, kj: (b, qi[t], 0)),
            scratch_shapes=[pltpu.VMEM((FOX_SEQS * N_HEADS, ROW_TILE, 128), F32),
                            pltpu.VMEM((FOX_SEQS * N_HEADS, ROW_TILE, 128), F32),
                            pltpu.VMEM((FOX_SEQS, ROW_TILE, GROUP), F32)]),
        out_shape=jax.ShapeDtypeStruct((batch, seq, GROUP), F32),
        compiler_params=_cparams("parallel", "arbitrary"),
        name="fox_attention",
    )(q_tile, k_tile, qa, ka, vb)
    return out.reshape(batch * seq, GROUP)


def _memkv_kernel(mem_ref, g_ref, b_ref, w_ref, o_ref, memn_ref):
    @pl.when(pl.program_id(0) == 0)
    def _():
        memn_ref[...] = _layernorm(mem_ref[...], g_ref[...], b_ref[...]).astype(BF16)

    o_ref[...] = jnp.dot(memn_ref[...], w_ref[...], preferred_element_type=F32).astype(BF16)


def _memkv(mem, g, b, w_kv):
    rows = mem.shape[0]
    n = w_kv.shape[0]
    return pl.pallas_call(
        _memkv_kernel,
        grid=(n,),
        in_specs=[pl.BlockSpec((rows, D_MODEL), lambda i: (0, 0)),
                  pl.BlockSpec((1, D_MODEL), lambda i: (0, 0)),
                  pl.BlockSpec((1, D_MODEL), lambda i: (0, 0)),
                  pl.BlockSpec((None, D_MODEL, D_MODEL), lambda i: (i, 0, 0))],
        out_specs=pl.BlockSpec((None, rows, D_MODEL), lambda i: (i, 0, 0)),
        out_shape=jax.ShapeDtypeStruct((n, rows, D_MODEL), BF16),
        scratch_shapes=[pltpu.VMEM((rows, D_MODEL), BF16)],
        compiler_params=_cparams("arbitrary"),
        name="memory_kv",
    )(mem, g, b, w_kv)


def _mix_cross_kernel(ya_ref, yb_ref, yc_ref, yd_ref, wm_ref, x_ref, g1_ref, b1_ref,
                      k_ref, v_ref, wq_ref, wo_ref, g_ref, b_ref, o_ref):
    y = jnp.concatenate([r[...].astype(BF16) for r in (ya_ref, yb_ref, yc_ref, yd_ref)], axis=1)
    mix = jnp.dot(y, wm_ref[...], preferred_element_type=F32)
    x = _layernorm(DN_ALPHA * x_ref[...] + mix, g1_ref[...], b1_ref[...])
    q = jnp.dot(x.astype(BF16), wq_ref[...], preferred_element_type=F32).astype(BF16)
    heads = range(CA_HEADS)
    cols = [slice(h * CA_HEAD_DIM, (h + 1) * CA_HEAD_DIM) for h in heads]
    s = [_dot_nt(q[:, cols[h]], k_ref[:, cols[h]]) * (CA_HEAD_DIM ** -0.5 * LOG2_E) for h in heads]
    p = [jnp.exp2(s[h] - jnp.max(s[h], axis=-1, keepdims=True)) for h in heads]
    p = [p[h] * (1.0 / jnp.sum(p[h], axis=-1, keepdims=True)) for h in heads]
    o = [jnp.dot(p[h].astype(BF16), v_ref[:, cols[h]], preferred_element_type=F32).astype(BF16)
         for h in heads]
    ca = jnp.dot(jnp.concatenate(o, axis=1), wo_ref[...], preferred_element_type=F32)
    o_ref[...] = _layernorm(DN_ALPHA * x + ca, g_ref[...], b_ref[...])


def _mix_cross(ys, x, kv, p, layer, batch, seq):
    t = x.shape[0]
    tiles = seq // ROW_TILE
    vec = pl.BlockSpec((None, 1, D_MODEL), lambda bb, i: (layer, 0, 0))
    mat = pl.BlockSpec((None, D_MODEL, D_MODEL), lambda bb, i: (layer, 0, 0))
    y_spec = pl.BlockSpec((ROW_TILE, GROUP), lambda bb, i: (bb * tiles + i, 0))
    x_spec = pl.BlockSpec((ROW_TILE, D_MODEL), lambda bb, i: (bb * tiles + i, 0))
    return pl.pallas_call(
        _mix_cross_kernel,
        grid=(batch, tiles),
        in_specs=[y_spec, y_spec, y_spec, y_spec, mat, x_spec, vec, vec,
                  pl.BlockSpec((None, N_MEM, D_MODEL), lambda bb, i: (2 * layer, bb, 0)),
                  pl.BlockSpec((None, N_MEM, D_MODEL), lambda bb, i: (2 * layer + 1, bb, 0)),
                  mat, mat, vec, vec],
        out_specs=x_spec,
        out_shape=jax.ShapeDtypeStruct((t, D_MODEL), F32),
        compiler_params=_cparams("parallel", "parallel"),
        name="mix_cross_attention",
    )(*ys, p["w_out"], x, p["ln1_g"], p["ln1_b"], kv, kv, p["ca_wq"], p["ca_wo"],
      p["ln2_g"], p["ln2_b"])


def _ffn_kernel(x_ref, wu_ref, bu_ref, cw_ref, cb_ref, wd_ref, g_ref, b_ref, o_ref,
                act_ref, carry_ref, *, tiles_per_seq):
    tm = FFN_ROW_TILE
    tf = FFN_COL_TILE
    first = (pl.program_id(0) % tiles_per_seq) == 0

    @pl.when(first)
    def _():
        carry_ref[...] = jnp.broadcast_to(-bu_ref[...], carry_ref.shape)

    x = x_ref[...]
    xb = x.astype(BF16)
    row8 = lax.broadcasted_iota(jnp.int32, (8, tf), 0)

    def conv(col, scale):
        cols = slice(col, col + tf)
        h = jnp.dot(xb, wu_ref[:, cols], preferred_element_type=F32)
        prev = carry_ref[:, cols]
        carry_ref[:, cols] = h[tm - 8:tm, :]
        h1 = pltpu.roll(h, 1, 0)
        h2 = pltpu.roll(h, 2, 0)
        top1 = jnp.where(row8 == 0, prev[7:8, :], h1[0:8, :])
        top2 = jnp.where(row8 == 0, prev[6:7, :], jnp.where(row8 == 1, prev[7:8, :], h2[0:8, :]))
        h1 = jnp.concatenate([top1, h1[8:tm, :]], axis=0)
        h2 = jnp.concatenate([top2, h2[8:tm, :]], axis=0)
        cw = cw_ref[:, cols]
        bias = (cb_ref[:, cols] + bu_ref[:, cols] * (cw[0:1, :] + cw[1:2, :] + cw[2:3, :])) * scale
        cw = cw * scale
        return cw[0:1, :] * h2 + cw[1:2, :] * h1 + cw[2:3, :] * h + bias

    for c in range(D_FF // tf):
        gate = conv(c * tf, 2.0 ** -0.5)
        val = conv(D_FF + c * tf, 2.0 ** 0.5)
        act = gate * (1.0 + lax.erf(gate)) * val
        act_ref[:, c * tf:(c + 1) * tf] = act.astype(BF16)

    ff = jnp.dot(act_ref[...], wd_ref[...], preferred_element_type=F32)
    o_ref[...] = _layernorm(DN_ALPHA * x + ff, g_ref[...], b_ref[...])


def _ffn(x, p, layer, seq):
    t = x.shape[0]
    once = pl.Buffered(1)
    vec = pl.BlockSpec((None, 1, D_MODEL), lambda i: (layer, 0, 0))
    wide = lambda rows: pl.BlockSpec((None, rows, 2 * D_FF), lambda i: (layer, 0, 0))
    return pl.pallas_call(
        functools.partial(_ffn_kernel, tiles_per_seq=seq // FFN_ROW_TILE),
        grid=(t // FFN_ROW_TILE,),
        in_specs=[pl.BlockSpec((FFN_ROW_TILE, D_MODEL), lambda i: (i, 0)),
                  pl.BlockSpec((None, D_MODEL, 2 * D_FF), lambda i: (layer, 0, 0), pipeline_mode=once),
                  wide(1), wide(3), wide(1),
                  pl.BlockSpec((None, D_FF, D_MODEL), lambda i: (layer, 0, 0), pipeline_mode=once),
                  vec, vec],
        out_specs=pl.BlockSpec((FFN_ROW_TILE, D_MODEL), lambda i: (i, 0)),
        out_shape=jax.ShapeDtypeStruct((t, D_MODEL), F32),
        scratch_shapes=[pltpu.VMEM((FFN_ROW_TILE, D_FF), BF16), pltpu.VMEM((8, 2 * D_FF), F32)],
        compiler_params=_cparams("arbitrary"),
        name="conv_ffn_ln",
    )(x, p["ffn_up"], p["ffn_up_b"], p["ffn_conv"], p["ffn_conv_b"], p["ffn_down"],
      p["ln3_g"], p["ln3_b"])


def _pack_params(w_in, w_out, sg_ln_g, sg_ln_b, sg_w, sg_b, rw_mu, rw_w0, rw_w2, rw_a0, rw_a2, rw_g2,
                 rw_kk, rw_ka, rw_rk, rw_lnx_g, rw_lnx_b, gla_a_up, gla_a_b, gla_norm_g, fox_fb,
                 ln1_g, ln1_b, ca_wq, ca_wk, ca_wv, ca_wo, ln2_g, ln2_b,
                 ffn_up, ffn_up_b, ffn_conv, ffn_conv_b, ffn_down, ln3_g, ln3_b):
    depth = w_in.shape[0]
    row = lambda a: a.reshape(depth, 1, -1)

    def pad_rows(w, start):
        out = jnp.zeros((depth, GROUP, GROUP), F32)
        return out.at[:, start:start + w.shape[1], :].set(w).astype(BF16)

    fb = jnp.zeros((depth, 1, 128), F32).at[:, 0, MISC_FOX_LANE:MISC_FOX_LANE + 4].set(fox_fb)
    aup = jnp.zeros((depth, 128, N_HEADS * GLA_DK), F32).at[:, 0:16, :].set(gla_a_up)
    return {
        "w_in": w_in.astype(BF16),
        "w_out": w_out.astype(BF16),
        "sg_ln_g": row(sg_ln_g), "sg_ln_b": row(sg_ln_b), "sg_w": sg_w,
        "sg_bias": jnp.repeat(jnp.swapaxes(sg_b, 1, 2), HEAD_DIM, axis=2),
        "rw_mu": rw_mu.reshape(depth, 4, 1, GROUP),
        "rw_w0": row(rw_w0), "rw_a0": row(rw_a0),
        "rw_w2": pad_rows(rw_w2, 0), "rw_a2": pad_rows(rw_a2, 64), "rw_g2": pad_rows(rw_g2, 128),
        "rw_kk": row(rw_kk), "rw_ka": row(rw_ka), "rw_rk": row(rw_rk),
        "rw_lnx_g": row(rw_lnx_g), "rw_lnx_b": row(rw_lnx_b),
        "gla_a_up": aup.astype(BF16), "gla_a_b": row(gla_a_b), "gla_norm_g": row(gla_norm_g),
        "fox_fb": fb,
        "ln1_g": row(ln1_g), "ln1_b": row(ln1_b),
        "ca_wq": ca_wq.astype(BF16), "ca_wo": ca_wo.astype(BF16),
        "ca_wkv": jnp.stack([ca_wk, ca_wv], axis=1).reshape(2 * depth, D_MODEL, D_MODEL).astype(BF16),
        "ln2_g": row(ln2_g), "ln2_b": row(ln2_b),
        "ffn_up": ffn_up.astype(BF16), "ffn_up_b": row(ffn_up_b), "ffn_conv": ffn_conv,
        "ffn_conv_b": row(ffn_conv_b),
        "ffn_down": (0.5 * ffn_down).astype(BF16),
        "ln3_g": row(ln3_g), "ln3_b": row(ln3_b),
    }


def _mixer_outputs(x, p, layer, batch, seq):
    z = _inproj(x, p["w_in"], layer)
    y_a, y_b, y_c = _token_mixers(z, p, layer, batch, seq)
    y_d = _fox(*_fox_prep(z, p["fox_fb"], layer, batch, seq), batch, seq)
    return y_a, y_b, y_c, y_d


def kernel(x, mem, mem_ln_g, mem_ln_b, w_in, w_out, sg_ln_g, sg_ln_b, sg_w, sg_b, rw_mu, rw_w0, rw_w2, rw_a0, rw_a2, rw_g2, rw_kk, rw_ka, rw_rk, rw_lnx_g, rw_lnx_b, gla_a_up, gla_a_b, gla_norm_g, fox_fb, ln1_g, ln1_b, ca_wq, ca_wk, ca_wv, ca_wo, ln2_g, ln2_b, ffn_up, ffn_up_b, ffn_conv, ffn_conv_b, ffn_down, ln3_g, ln3_b):
    batch, seq, _ = x.shape
    depth = w_in.shape[0]
    p = _pack_params(w_in, w_out, sg_ln_g, sg_ln_b, sg_w, sg_b, rw_mu, rw_w0, rw_w2, rw_a0, rw_a2,
                     rw_g2, rw_kk, rw_ka, rw_rk, rw_lnx_g, rw_lnx_b, gla_a_up, gla_a_b, gla_norm_g,
                     fox_fb, ln1_g, ln1_b, ca_wq, ca_wk, ca_wv, ca_wo, ln2_g, ln2_b,
                     ffn_up, ffn_up_b, ffn_conv, ffn_conv_b, ffn_down, ln3_g, ln3_b)
    kv = _memkv(mem.reshape(batch * N_MEM, D_MODEL), mem_ln_g.reshape(1, -1), mem_ln_b.reshape(1, -1),
                p["ca_wkv"])
    h = x.reshape(batch * seq, D_MODEL)
    for layer in range(depth):
        ys = _mixer_outputs(h, p, layer, batch, seq)
        h = _mix_cross(ys, h, kv, p, layer, batch, seq)
        h = _ffn(h, p, layer, seq)
    return h.reshape(batch, seq, D_MODEL)
```
